```python
import jax, jax.numpy as jnp
from jax import lax
import numpy as np

D_MODEL = 1024
BATCH = 8
SEQ = 8192
DEPTH = 4
DEC_BATCH = 32
DEC_SEQ = 32
PAST_LEN = 2048

CHUNK = 64
Q_BLOCK = 128
D_FF = 2816
GLA_HEADS = 4
GLA_DK = D_MODEL // 2 // GLA_HEADS
GLA_DV = D_MODEL // GLA_HEADS
GLA_RANK = 16
GLA_TAU = 16.0
FOX_HEADS = 8
FOX_HD = D_MODEL // FOX_HEADS
FOX_F_BIAS = 3.0
MEM_LEN = 256
MEM_HEADS = 4
MEM_HD = D_MODEL // MEM_HEADS
N_BRANCH = 3
EPS = 1e-6
IN_SIZES = (GLA_HEADS * GLA_DK, GLA_HEADS * GLA_DK, GLA_HEADS * GLA_DV, GLA_RANK, GLA_HEADS * GLA_DV,
            FOX_HEADS * FOX_HD, FOX_HEADS * FOX_HD, FOX_HEADS * FOX_HD, FOX_HEADS,
            MEM_HEADS * MEM_HD, N_BRANCH * D_MODEL)
D_IN = sum(IN_SIZES)

kernel_name = 'gated_hybrid_gla_fox_memory_encoder_step'


def rmsnorm(x, g):
    xf = x.astype(jnp.float32)
    y = xf * lax.rsqrt(jnp.mean(xf * xf, axis=-1, keepdims=True) + EPS)
    return (y * g.astype(jnp.float32)).astype(x.dtype)


def ffn_half(x, pre_g, post_g, wg, wu, wd):
    u = rmsnorm(x, pre_g)
    h = (jax.nn.silu(u @ wg) * (u @ wu)) @ wd
    return x + 0.5 * rmsnorm(h, post_g)


def mixer_inputs(u, w_in, gla_w_a2, gla_b_a, fox_b_f):
    B, T, _ = u.shape
    splits = np.cumsum(IN_SIZES)[:-1].tolist()
    gq, gk, gv, glr, gr, fq, fk, fv, ff, mq, gates = jnp.split(u @ w_in, splits, axis=-1)
    gq = gq.reshape(B, T, GLA_HEADS, GLA_DK) * (GLA_DK ** -0.5)
    gk = gk.reshape(B, T, GLA_HEADS, GLA_DK)
    gv = gv.reshape(B, T, GLA_HEADS, GLA_DV)
    g_log_a = jax.nn.log_sigmoid((glr @ gla_w_a2 + gla_b_a).astype(jnp.float32)) / GLA_TAU
    g_log_a = g_log_a.reshape(B, T, GLA_HEADS, GLA_DK)
    fq = fq.reshape(B, T, FOX_HEADS, FOX_HD)
    fk = fk.reshape(B, T, FOX_HEADS, FOX_HD)
    fv = fv.reshape(B, T, FOX_HEADS, FOX_HD)
    f_log = jax.nn.log_sigmoid((ff + fox_b_f).astype(jnp.float32))
    mq = mq.reshape(B, T, MEM_HEADS, MEM_HD)
    gates = jax.nn.sigmoid(gates.reshape(B, T, N_BRANCH, D_MODEL))
    return gq, gk, gv, g_log_a, gr, fq, fk, fv, f_log, mq, gates


def gla_chunk(S, q, k, v, log_a):
    L = q.shape[1]
    b = jnp.cumsum(log_a.astype(jnp.float32), axis=1)
    o_inter = jnp.einsum('blhk,bhkv->blhv', q * jnp.exp(b), S)
    causal = jnp.tril(jnp.ones((L, L), dtype=bool))
    rel = jnp.where(causal[None, :, :, None, None], b[:, :, None] - b[:, None, :], -jnp.inf)
    attn = jnp.einsum('bthk,bshk,btshk->bhts', q, k, jnp.exp(rel))
    o_intra = jnp.einsum('bhts,bshv->bthv', attn, v)
    b_end = b[:, -1]
    S_new = jnp.exp(b_end)[..., None] * S + jnp.einsum('bshk,bshv->bhkv', k * jnp.exp(b_end[:, None] - b), v)
    return S_new, o_inter + o_intra


def gla_prompt(q, k, v, log_a):
    B, T = q.shape[:2]
    n_c = T // CHUNK

    def to_chunks(a):
        return jnp.swapaxes(a.reshape(B, n_c, CHUNK, *a.shape[2:]), 0, 1)

    S0 = jnp.zeros((B, GLA_HEADS, GLA_DK, GLA_DV), jnp.float32)
    S_fin, o = lax.scan(lambda S, xs: gla_chunk(S, xs[0], xs[1], xs[2], xs[3]), S0,
                        (to_chunks(q), to_chunks(k), to_chunks(v), to_chunks(log_a)))
    return S_fin, jnp.swapaxes(o, 0, 1).reshape(B, T, GLA_HEADS, GLA_DV)


def gla_branch(o, r, norm_g, w_o):
    B, T = o.shape[:2]
    on = rmsnorm(o, norm_g.reshape(GLA_HEADS, GLA_DV)).reshape(B, T, GLA_HEADS * GLA_DV)
    return (on.astype(r.dtype) * jax.nn.silu(r)) @ w_o


def fox_attend(q, fq_cum, pos_q, k, v, fk_cum, pos_k):
    s = jnp.einsum('bqhd,bkhd->bhqk', q, k).astype(jnp.float32) * (FOX_HD ** -0.5)
    s = s + jnp.swapaxes(fq_cum, 1, 2)[..., :, None] - jnp.swapaxes(fk_cum, 1, 2)[..., None, :]
    s = jnp.where(pos_k[None, :] <= pos_q[:, None], s, -jnp.inf)
    p = jax.nn.softmax(s, axis=-1)
    return jnp.einsum('bhqk,bkhd->bqhd', p.astype(v.dtype), v)


def fox_prompt(q, k, v, f_log):
    B, T = q.shape[:2]
    n_b = T // Q_BLOCK
    F = jnp.cumsum(f_log, axis=1)
    pos = jnp.arange(T)

    def blocks(a):
        return jnp.swapaxes(a.reshape(B, n_b, Q_BLOCK, *a.shape[2:]), 0, 1)

    o = lax.map(lambda xs: fox_attend(xs[0], xs[1], xs[2], k, v, F, pos),
                (blocks(q), blocks(F), pos.reshape(n_b, Q_BLOCK)))
    return jnp.swapaxes(o, 0, 1).reshape(B, T, FOX_HEADS * FOX_HD)


def fox_sample(q, k_new, v_new, f_new, cache_k, cache_v, cache_f):
    B, L = q.shape[:2]
    P = cache_k.shape[1]
    k = jnp.concatenate([cache_k, k_new.astype(cache_k.dtype)], axis=1)
    v = jnp.concatenate([cache_v, v_new.astype(cache_v.dtype)], axis=1)
    F = jnp.cumsum(jnp.concatenate([cache_f.astype(jnp.float32), f_new], axis=1), axis=1)
    pos = jnp.arange(P + L)
    o = fox_attend(q, F[:, P:], pos[P:], k, v, F, pos)
    return o.reshape(B, L, FOX_HEADS * FOX_HD)


def mem_kv(mem, g, w):
    B, M, _ = mem.shape
    k, v = jnp.split(rmsnorm(mem, g) @ w, 2, axis=-1)
    return k.reshape(B, M, MEM_HEADS, MEM_HD), v.reshape(B, M, MEM_HEADS, MEM_HD)


def cross_attend(q, k, v):
    B, T = q.shape[:2]
    s = jnp.einsum('bqhd,bkhd->bhqk', q, k).astype(jnp.float32) * (MEM_HD ** -0.5)
    p = jax.nn.softmax(s, axis=-1)
    return jnp.einsum('bhqk,bkhd->bqhd', p.astype(v.dtype), v).reshape(B, T, MEM_HEADS * MEM_HD)


def mixer_out(gates, o_gla, gr, o_fox, o_mem, gla_norm_g, w_gla_o, w_fox_o, w_mem_o, w_out, dtype):
    b_gla = gla_branch(o_gla, gr, gla_norm_g, w_gla_o)
    y = gates[:, :, 0] * b_gla + gates[:, :, 1] * (o_fox @ w_fox_o) + gates[:, :, 2] * (o_mem @ w_mem_o)
    return (y @ w_out).astype(dtype)


def setup_inputs(seed: int = 0) -> dict:
    key = jax.random.key(seed)
    ks = iter(jax.random.split(key, 48))

    def nrm(shape, scale):
        return jax.random.normal(next(ks), shape, jnp.float32) * scale

    def gain(shape):
        return 1.0 + nrm(shape, 0.05)

    L = DEPTH
    return {
        'x_prompt': nrm((BATCH, SEQ, D_MODEL), 1.0),
        'x_sample': nrm((DEC_BATCH, DEC_SEQ, D_MODEL), 1.0),
        'cache_fox_k': nrm((L, DEC_BATCH, PAST_LEN, FOX_HEADS, FOX_HD), 1.0),
        'cache_fox_v': nrm((L, DEC_BATCH, PAST_LEN, FOX_HEADS, FOX_HD), 1.0),
        'cache_fox_logf': jax.nn.log_sigmoid(FOX_F_BIAS + nrm((L, DEC_BATCH, PAST_LEN, FOX_HEADS), 1.0)),
        'state_gla': nrm((L, DEC_BATCH, GLA_HEADS, GLA_DK, GLA_DV), 1.0),
        'cache_mem_k': nrm((L, DEC_BATCH, MEM_LEN, MEM_HEADS, MEM_HD), 1.0),
        'cache_mem_v': nrm((L, DEC_BATCH, MEM_LEN, MEM_HEADS, MEM_HD), 1.0),
        'mem_prompt': nrm((BATCH, MEM_LEN, D_MODEL), 1.0),
        'ffn1_pre_g': gain((L, D_MODEL)),
        'ffn1_post_g': gain((L, D_MODEL)),
        'ffn1_w_gate': nrm((L, D_MODEL, D_FF), D_MODEL ** -0.5),
        'ffn1_w_up': nrm((L, D_MODEL, D_FF), D_MODEL ** -0.5),
        'ffn1_w_down': nrm((L, D_FF, D_MODEL), D_FF ** -0.5),
        'mix_pre_g': gain((L, D_MODEL)),
        'mix_post_g': gain((L, D_MODEL)),
        'w_in': nrm((L, D_MODEL, D_IN), D_MODEL ** -0.5),
        'gla_w_a2': nrm((L, GLA_RANK, GLA_HEADS * GLA_DK), GLA_RANK ** -0.5),
        'gla_b_a': nrm((L, GLA_HEADS * GLA_DK), 0.1),
        'fox_b_f': FOX_F_BIAS + nrm((L, FOX_HEADS), 0.1),
        'gla_norm_g': gain((L, GLA_HEADS * GLA_DV)),
        'w_gla_o': nrm((L, GLA_HEADS * GLA_DV, D_MODEL), (GLA_HEADS * GLA_DV) ** -0.5),
        'w_fox_o': nrm((L, FOX_HEADS * FOX_HD, D_MODEL), (FOX_HEADS * FOX_HD) ** -0.5),
        'w_mem_o': nrm((L, MEM_HEADS * MEM_HD, D_MODEL), (MEM_HEADS * MEM_HD) ** -0.5),
        'w_out': nrm((L, D_MODEL, D_MODEL), D_MODEL ** -0.5),
        'mem_norm_g': gain((L, D_MODEL)),
        'w_mem_kv': nrm((L, D_MODEL, 2 * MEM_HEADS * MEM_HD), D_MODEL ** -0.5),
        'ffn2_pre_g': gain((L, D_MODEL)),
        'ffn2_post_g': gain((L, D_MODEL)),
        'ffn2_w_gate': nrm((L, D_MODEL, D_FF), D_MODEL ** -0.5),
        'ffn2_w_up': nrm((L, D_MODEL, D_FF), D_MODEL ** -0.5),
        'ffn2_w_down': nrm((L, D_FF, D_MODEL), D_FF ** -0.5),
    }


def reference(x_prompt, x_sample, cache_fox_k, cache_fox_v, cache_fox_logf, state_gla, cache_mem_k, cache_mem_v,
              mem_prompt, ffn1_pre_g, ffn1_post_g, ffn1_w_gate, ffn1_w_up, ffn1_w_down, mix_pre_g, mix_post_g,
              w_in, gla_w_a2, gla_b_a, fox_b_f, gla_norm_g, w_gla_o, w_fox_o, w_mem_o, w_out, mem_norm_g,
              w_mem_kv, ffn2_pre_g, ffn2_post_g, ffn2_w_gate, ffn2_w_up, ffn2_w_down):
    x = x_prompt
    pk, pv, pf, ps, pmk, pmv = [], [], [], [], [], []
    for l in range(DEPTH):
        x = ffn_half(x, ffn1_pre_g[l], ffn1_post_g[l], ffn1_w_gate[l], ffn1_w_up[l], ffn1_w_down[l])
        u = rmsnorm(x, mix_pre_g[l])
        gq, gk, gv, g_log_a, gr, fq, fk, fv, f_log, mq, gates = mixer_inputs(u, w_in[l], gla_w_a2[l], gla_b_a[l], fox_b_f[l])
        S_fin, o_gla = gla_prompt(gq, gk, gv, g_log_a)
        o_fox = fox_prompt(fq, fk, fv, f_log)
        mk, mv = mem_kv(mem_prompt, mem_norm_g[l], w_mem_kv[l])
        o_mem = cross_attend(mq, mk, mv)
        mixed = mixer_out(gates, o_gla, gr, o_fox, o_mem, gla_norm_g[l], w_gla_o[l], w_fox_o[l], w_mem_o[l], w_out[l], x.dtype)
        x = x + rmsnorm(mixed, mix_post_g[l])
        x = ffn_half(x, ffn2_pre_g[l], ffn2_post_g[l], ffn2_w_gate[l], ffn2_w_up[l], ffn2_w_down[l])
        pk.append(fk); pv.append(fv); pf.append(f_log); ps.append(S_fin); pmk.append(mk); pmv.append(mv)
    y_prompt = x

    x = x_sample
    sk, sv, sf, ss = [], [], [], []
    for l in range(DEPTH):
        x = ffn_half(x, ffn1_pre_g[l], ffn1_post_g[l], ffn1_w_gate[l], ffn1_w_up[l], ffn1_w_down[l])
        u = rmsnorm(x, mix_pre_g[l])
        gq, gk, gv, g_log_a, gr, fq, fk, fv, f_log, mq, gates = mixer_inputs(u, w_in[l], gla_w_a2[l], gla_b_a[l], fox_b_f[l])
        S_new, o_gla = gla_chunk(state_gla[l].astype(jnp.float32), gq, gk, gv, g_log_a)
        o_fox = fox_sample(fq, fk, fv, f_log, cache_fox_k[l], cache_fox_v[l], cache_fox_logf[l])
        o_mem = cross_attend(mq, cache_mem_k[l], cache_mem_v[l])
        mixed = mixer_out(gates, o_gla, gr, o_fox, o_mem, gla_norm_g[l], w_gla_o[l], w_fox_o[l], w_mem_o[l], w_out[l], x.dtype)
        x = x + rmsnorm(mixed, mix_post_g[l])
        x = ffn_half(x, ffn2_pre_g[l], ffn2_post_g[l], ffn2_w_gate[l], ffn2_w_up[l], ffn2_w_down[l])
        sk.append(fk); sv.append(fv); sf.append(f_log); ss.append(S_new)
    y_sample = x

    p_fox_k = jnp.stack(pk)
    p_fox_v = jnp.stack(pv)
    p_fox_logf = jnp.stack(pf)
    p_state_gla = jnp.stack(ps)
    p_mem_k = jnp.stack(pmk)
    p_mem_v = jnp.stack(pmv)
    s_fox_k = jnp.stack(sk)
    s_fox_v = jnp.stack(sv)
    s_fox_logf = jnp.stack(sf)
    s_state_gla = jnp.stack(ss)
    return (y_prompt, y_sample, p_fox_k, p_fox_v, p_fox_logf, p_state_gla, p_mem_k, p_mem_v,
            s_fox_k, s_fox_v, s_fox_logf, s_state_gla)
```

```python
import functools

import numpy as np
import jax
import jax.numpy as jnp
from jax import lax
from jax.experimental import pallas as pl
from jax.experimental.pallas import tpu as pltpu

F32 = jnp.float32
BF16 = jnp.bfloat16

EPS = 1e-6
GLA_HEADS = 4
GLA_RANK = 16
GLA_TAU = 16.0
FOX_HEADS = 8
MEM_HEADS = 4
N_BRANCH = 3

LANES = 128
V7X_VMEM_LIMIT = 56 * 1024 * 1024
FFN_CHUNK = 256
GLA_CHUNK = 64
GLA_SUB = 16
SMALL_W = 128


def _dot(a, b):
    return jnp.dot(a, b, preferred_element_type=F32)


def _dot_nt(a, b):
    return lax.dot_general(a, b, (((1,), (1,)), ((), ())), preferred_element_type=F32)


def _dot_tn(a, b):
    return lax.dot_general(a, b, (((0,), (0,)), ((), ())), preferred_element_type=F32)


def _rmsnorm(x, g):
    return x * lax.rsqrt(jnp.mean(x * x, axis=-1, keepdims=True) + EPS) * g


def _sigmoid(x):
    return 1.0 / (1.0 + jnp.exp(-x))


def _log_sigmoid(x):
    return jnp.minimum(x, 0.0) - jnp.log1p(jnp.exp(-jnp.abs(x)))


def _split3(x):
    hi = x.astype(BF16)
    r1 = x - hi.astype(F32)
    mid = r1.astype(BF16)
    lo = (r1 - mid.astype(F32)).astype(BF16)
    return hi, mid, lo


def _resident(shape):
    nd = len(shape)
    return pl.BlockSpec(shape, lambda *_: (0,) * nd, pipeline_mode=pl.Buffered(1))


def _params(semantics):
    return pltpu.CompilerParams(dimension_semantics=semantics, vmem_limit_bytes=V7X_VMEM_LIMIT)


def _row_tile(m, want):
    t = min(m, want)
    assert m % t == 0, (m, t)
    return t


def _ffn_body(x_ref, pre_ref, post_ref, wg_ref, wu_ref, wd_ref, o_ref, u_ref, acc_ref):
    x = x_ref[...]
    u_ref[...] = _rmsnorm(x, pre_ref[...]).astype(BF16)
    for c in range(wg_ref.shape[0]):
        u = u_ref[...]
        g = _dot(u, wg_ref[c])
        up = _dot(u, wu_ref[c])
        h = (g * _sigmoid(g) * up).astype(BF16)
        part = _dot(h, wd_ref[c])
        if c == 0:
            acc_ref[...] = part
        else:
            acc_ref[...] += part
    o_ref[...] = x + 0.5 * _rmsnorm(acc_ref[...], post_ref[...])


def _ffn(x2, pre_g, post_g, wg, wu, wd, tm=512):
    m, d = x2.shape
    tm = _row_tile(m, tm)
    return pl.pallas_call(
        _ffn_body,
        grid=(m // tm,),
        in_specs=[pl.BlockSpec((tm, d), lambda i: (i, 0)),
                  _resident(pre_g.shape), _resident(post_g.shape),
                  _resident(wg.shape), _resident(wu.shape), _resident(wd.shape)],
        out_specs=pl.BlockSpec((tm, d), lambda i: (i, 0)),
        out_shape=jax.ShapeDtypeStruct((m, d), F32),
        scratch_shapes=[pltpu.VMEM((tm, d), BF16), pltpu.VMEM((tm, d), F32)],
        compiler_params=_params(("parallel",)),
        name="ffn",
    )(x2, pre_g, post_g, wg, wu, wd)


def _mix_in_body(dims, x_ref, pre_ref, w_ref, ws_ref, wa2_ref, ba_ref, bf_ref,
                 gq_ref, gk_ref, gv_ref, la_ref, fq_ref, fk32_ref, fv32_ref, fk16_ref, fv16_ref, fl_ref, u_ref):
    gk_w, gv_w, fx_w, dk_scale, hd_scale = dims
    u_ref[...] = _rmsnorm(x_ref[...], pre_ref[...]).astype(BF16)

    def proj(lo, width):
        return _dot(u_ref[...], w_ref[:, lo:lo + width])

    off = 0
    gq_ref[...] = proj(off, gk_w) * dk_scale
    off += gk_w
    gk_ref[...] = proj(off, gk_w)
    off += gk_w
    gv_ref[...] = proj(off, gv_w).astype(BF16)
    off += gv_w
    fq_ref[...] = (proj(off, fx_w) * hd_scale).astype(BF16)
    off += fx_w
    fk = proj(off, fx_w)
    fk32_ref[...] = fk
    fk16_ref[...] = fk.astype(BF16)
    off += fx_w
    fv = proj(off, fx_w)
    fv32_ref[...] = fv
    fv16_ref[...] = fv.astype(BF16)

    small = _dot(u_ref[...], ws_ref[...])
    lane = lax.broadcasted_iota(jnp.int32, small.shape, 1)
    fl_ref[...] = jnp.where(lane < FOX_HEADS, _log_sigmoid(small + bf_ref[...]), 0.0)
    lr = jnp.where((lane >= FOX_HEADS) & (lane < FOX_HEADS + GLA_RANK), small, 0.0).astype(BF16)
    la_ref[...] = _log_sigmoid(_dot(lr, wa2_ref[...]) + ba_ref[...]) * (1.0 / GLA_TAU)


def _mix_in(x2, pre_g, w1, ws, wa2p, b_a, b_f, tm=512):
    m, d = x2.shape
    tm = _row_tile(m, tm)
    gk_w = d // 2
    dims = (gk_w, d, d, float((gk_w // GLA_HEADS) ** -0.5), float((d // FOX_HEADS) ** -0.5))

    def rows(width):
        return pl.BlockSpec((tm, width), lambda i: (i, 0))

    outs = [(gk_w, F32), (gk_w, F32), (d, BF16), (gk_w, F32), (d, BF16), (d, F32), (d, F32), (d, BF16), (d, BF16),
            (SMALL_W, F32)]
    return pl.pallas_call(
        functools.partial(_mix_in_body, dims),
        grid=(m // tm,),
        in_specs=[rows(d), _resident(pre_g.shape), _resident(w1.shape), _resident(ws.shape),
                  _resident(wa2p.shape), _resident(b_a.shape), _resident(b_f.shape)],
        out_specs=[rows(w) for w, _ in outs],
        out_shape=[jax.ShapeDtypeStruct((m, w), dt) for w, dt in outs],
        scratch_shapes=[pltpu.VMEM((tm, d), BF16)],
        compiler_params=_params(("parallel",)),
        name="mix_in",
    )(x2, pre_g, w1, ws, wa2p, b_a, b_f)


def _cumsum_body(x_ref, m_ref, o_ref):
    x = x_ref[...]
    r = lax.broadcasted_iota(jnp.int32, (LANES, LANES), 0)
    c = lax.broadcasted_iota(jnp.int32, (LANES, LANES), 1)
    upper = jnp.where(r <= c, 1.0, 0.0).astype(BF16)
    hi, mid, lo = _split3(x)
    o_ref[...] = _dot(hi, upper) + _dot(mid, upper) + _dot(lo, upper)
    tot = jnp.broadcast_to(o_ref[:, LANES - 1:LANES], x.shape)
    hi, mid, lo = _split3(tot)
    mm = m_ref[...]
    offset = _dot(mm, hi) + _dot(mm, mid) + _dot(mm, lo)
    o_ref[...] = o_ref[...] + offset


def _cumsum_time(x3):
    g, n, _ = x3.shape
    per = 8
    while (per * n) % 8:
        per += 1
    gpb = per if g % per == 0 else g
    rb = gpb * n
    idx = np.arange(rb)
    earlier = (idx[:, None] // n == idx[None, :] // n) & (idx[None, :] < idx[:, None])
    mm = jnp.asarray(earlier.astype(np.float32), BF16)
    out = pl.pallas_call(
        _cumsum_body,
        grid=(g // gpb,),
        in_specs=[pl.BlockSpec((rb, LANES), lambda i: (i, 0)), _resident((rb, rb))],
        out_specs=pl.BlockSpec((rb, LANES), lambda i: (i, 0)),
        out_shape=jax.ShapeDtypeStruct((g * n, LANES), F32),
        compiler_params=_params(("parallel",)),
        name="cumsum",
    )(x3.reshape(g * n, LANES), mm)
    return out.reshape(g, n, LANES)


def _gla_chunk(q, k, v, la, st, place_ref):
    cl, dk = q.shape
    sub = min(GLA_SUB, cl)
    row = lax.broadcasted_iota(jnp.int32, (cl, cl), 0)
    col = lax.broadcasted_iota(jnp.int32, (cl, cl), 1)
    lower = jnp.where(row >= col, 1.0, 0.0).astype(BF16)
    hi, mid, lo = _split3(la)
    b = _dot(lower, hi) + _dot(lower, mid) + _dot(lower, lo)
    b_end = b[cl - 1:cl, :]

    o = _dot_nt((q * jnp.exp(b)).astype(BF16), st.astype(BF16))

    tok = lax.broadcasted_iota(jnp.int32, (cl, dk), 0)
    tsub = lax.broadcasted_iota(jnp.int32, (sub, dk), 0)
    blocks = []
    for i in range(cl // sub):
        r0 = i * sub
        b_i, q_i, k_i = b[r0:r0 + sub], q[r0:r0 + sub], k[r0:r0 + sub]
        cols = []
        for s in range(sub):
            rel = jnp.where(tsub >= s, b_i - b_i[s:s + 1], -jnp.inf)
            cols.append((q_i * k_i[s:s + 1] * jnp.exp(rel)).astype(BF16))
        a_i = _dot(jnp.concatenate(cols, axis=1), place_ref[i])
        if i > 0:
            ref = b[r0 - 1:r0]
            q_t = (q_i * jnp.exp(b_i - ref)).astype(BF16)
            k_t = (k * jnp.exp(jnp.where(tok < r0, ref - b, -jnp.inf))).astype(BF16)
            a_i = a_i + _dot_nt(q_t, k_t)
        blocks.append(a_i)
    attn = jnp.concatenate(blocks, axis=0).astype(BF16)
    o = o + _dot(attn, v)

    k_d = (k * jnp.exp(b_end - b)).astype(BF16)
    st_new = st * jnp.exp(b_end) + _dot_tn(v, k_d)
    return o, st_new


def _gla_body(cl, q_ref, k_ref, v_ref, la_ref, s0_ref, place_ref, o_ref, sf_ref, st_ref):
    t = pl.program_id(2)

    @pl.when(t == 0)
    def _():
        st_ref[...] = s0_ref[0, 0].T

    n_chunks = q_ref.shape[1] // cl

    def chunk(ci, carry):
        r = pl.multiple_of(ci * cl, cl)
        o, st = _gla_chunk(q_ref[0, pl.ds(r, cl), :], k_ref[0, pl.ds(r, cl), :], v_ref[0, pl.ds(r, cl), :],
                           la_ref[0, pl.ds(r, cl), :], st_ref[...], place_ref)
        o_ref[0, pl.ds(r, cl), :] = o.astype(o_ref.dtype)
        st_ref[...] = st
        return carry

    lax.fori_loop(0, n_chunks, chunk, 0)

    @pl.when(t == pl.num_programs(2) - 1)
    def _():
        sf_ref[0, 0] = st_ref[...].T


def _gla_place(cl, dk):
    sub = min(GLA_SUB, cl)
    place = np.zeros((cl // sub, sub * dk, cl), np.float32)
    for i in range(cl // sub):
        for s in range(sub):
            place[i, s * dk:(s + 1) * dk, i * sub + s] = 1.0
    return jnp.asarray(place, BF16)


def _gla(gq, gk, gv, la, s0, tt=1024):
    b, t, _ = gq.shape
    _, h, dk, dv = s0.shape
    cl = min(GLA_CHUNK, t)
    tt = _row_tile(t, tt)
    place = _gla_place(cl, dk)
    return pl.pallas_call(
        functools.partial(_gla_body, cl),
        grid=(b, h, t // tt),
        in_specs=[pl.BlockSpec((1, tt, dk), lambda i, j, n: (i, n, j)),
                  pl.BlockSpec((1, tt, dk), lambda i, j, n: (i, n, j)),
                  pl.BlockSpec((1, tt, dv), lambda i, j, n: (i, n, j)),
                  pl.BlockSpec((1, tt, dk), lambda i, j, n: (i, n, j)),
                  pl.BlockSpec((1, 1, dk, dv), lambda i, j, n: (i, j, 0, 0)),
                  _resident(place.shape)],
        out_specs=[pl.BlockSpec((1, tt, dv), lambda i, j, n: (i, n, j)),
                   pl.BlockSpec((1, 1, dk, dv), lambda i, j, n: (i, j, 0, 0))],
        out_shape=[jax.ShapeDtypeStruct((b, t, h * dv), BF16), jax.ShapeDtypeStruct(s0.shape, F32)],
        scratch_shapes=[pltpu.VMEM((dv, dk), F32)],
        compiler_params=_params(("parallel", "parallel", "arbitrary")),
        name="gla",
    )(gq, gk, gv, la, s0, place)


def _head_column(block, head):
    lane = lax.broadcasted_iota(jnp.int32, block.shape, 1)
    return jnp.sum(jnp.where(lane == head, block, 0.0), axis=1, keepdims=True)


def _fox_prompt_body(q_ref, k_ref, v_ref, fq_ref, fk_ref, o_ref, m_ref, l_ref, acc_ref):
    head = pl.program_id(1)
    qi = pl.program_id(2)
    tq = q_ref.shape[1]
    q = q_ref[0]
    f_q = _head_column(fq_ref[0], head)

    def logits(kb):
        r = pl.multiple_of(kb * tq, tq)
        s = _dot_nt(q, k_ref[0, pl.ds(r, tq), :])
        return s + f_q - fk_ref[0, 0, pl.ds(kb, 1), :], r

    z, r = logits(qi)
    row = lax.broadcasted_iota(jnp.int32, z.shape, 0)
    col = lax.broadcasted_iota(jnp.int32, z.shape, 1)
    z = jnp.where(col <= row, z, -jnp.inf)
    m = jnp.max(z, axis=1, keepdims=True)
    p = jnp.exp(z - m)
    m_ref[...] = m
    l_ref[...] = jnp.sum(p, axis=1, keepdims=True)
    acc_ref[...] = _dot(p.astype(BF16), v_ref[0, pl.ds(r, tq), :])

    def block(kb, carry):
        z, r = logits(kb)
        m_old = m_ref[...]
        m_new = jnp.maximum(m_old, jnp.max(z, axis=1, keepdims=True))
        alpha = jnp.exp(m_old - m_new)
        p = jnp.exp(z - m_new)
        m_ref[...] = m_new
        l_ref[...] = alpha * l_ref[...] + jnp.sum(p, axis=1, keepdims=True)
        acc_ref[...] = alpha * acc_ref[...] + _dot(p.astype(BF16), v_ref[0, pl.ds(r, tq), :])
        return carry

    lax.fori_loop(0, qi, block, 0)
    o_ref[0] = (acc_ref[...] / l_ref[...]).astype(o_ref.dtype)


def _fox_prompt(fq, fk, fv, f_cols, f_rows, tq=512):
    b, t, d = fq.shape
    h = f_cols.shape[2]
    hd = d // h
    nq = t // tq
    return pl.pallas_call(
        _fox_prompt_body,
        grid=(b, h, nq),
        in_specs=[pl.BlockSpec((1, tq, hd), lambda i, j, n: (i, n, j)),
                  pl.BlockSpec((1, t, hd), lambda i, j, n: (i, 0, j)),
                  pl.BlockSpec((1, t, hd), lambda i, j, n: (i, 0, j)),
                  pl.BlockSpec((1, tq, h), lambda i, j, n: (i, n, 0)),
                  pl.BlockSpec((1, 1, nq, tq), lambda i, j, n: (i, j, 0, 0))],
        out_specs=pl.BlockSpec((1, tq, hd), lambda i, j, n: (i, n, j)),
        out_shape=jax.ShapeDtypeStruct((b, t, d), BF16),
        scratch_shapes=[pltpu.VMEM((tq, 1), F32), pltpu.VMEM((tq, 1), F32), pltpu.VMEM((tq, hd), F32)],
        compiler_params=_params(("parallel", "parallel", "arbitrary")),
        name="fox_prompt",
    )(fq, fk, fv, f_cols, f_rows)


def _fox_sample_body(q_ref, kc_ref, vc_ref, kn_ref, vn_ref, fq_ref, fk_ref, o_ref):
    head = pl.program_id(1)
    q = q_ref[0]
    ln = q.shape[0]
    past = kc_ref.shape[1]
    f_q = _head_column(fq_ref[0], head)
    f_k = fk_ref[0, 0]
    z_c = _dot_nt(q, kc_ref[0].astype(BF16)) + f_q - f_k[:, :past]
    z_n = _dot_nt(q, kn_ref[0]) + f_q - f_k[:, past:past + ln]
    row = lax.broadcasted_iota(jnp.int32, z_n.shape, 0)
    col = lax.broadcasted_iota(jnp.int32, z_n.shape, 1)
    z_n = jnp.where(col <= row, z_n, -jnp.inf)
    m = jnp.maximum(jnp.max(z_c, axis=1, keepdims=True), jnp.max(z_n, axis=1, keepdims=True))
    p_c = jnp.exp(z_c - m)
    p_n = jnp.exp(z_n - m)
    denom = jnp.sum(p_c, axis=1, keepdims=True) + jnp.sum(p_n, axis=1, keepdims=True)
    o = _dot(p_c.astype(BF16), vc_ref[0].astype(BF16)) + _dot(p_n.astype(BF16), vn_ref[0])
    o_ref[0] = (o / denom).astype(o_ref.dtype)


def _fox_sample(fq, fk, fv, cache_k, cache_v, f_cols, f_rows):
    b, ln, d = fq.shape
    past = cache_k.shape[1]
    h = f_cols.shape[2]
    hd = d // h
    fw = f_rows.shape[3]
    return pl.pallas_call(
        _fox_sample_body,
        grid=(b, h),
        in_specs=[pl.BlockSpec((1, ln, hd), lambda i, j: (i, 0, j)),
                  pl.BlockSpec((1, past, hd), lambda i, j: (i, 0, j)),
                  pl.BlockSpec((1, past, hd), lambda i, j: (i, 0, j)),
                  pl.BlockSpec((1, ln, hd), lambda i, j: (i, 0, j)),
                  pl.BlockSpec((1, ln, hd), lambda i, j: (i, 0, j)),
                  pl.BlockSpec((1, ln, h), lambda i, j: (i, 0, 0)),
                  pl.BlockSpec((1, 1, 1, fw), lambda i, j: (i, j, 0, 0))],
        out_specs=pl.BlockSpec((1, ln, hd), lambda i, j: (i, 0, j)),
        out_shape=jax.ShapeDtypeStruct((b, ln, d), BF16),
        compiler_params=_params(("parallel", "parallel")),
        name="fox_sample",
    )(fq, cache_k, cache_v, fk, fv, f_cols, f_rows)


def _mem_kv_body(mem_ref, g_ref, w_ref, k_ref, v_ref):
    u = _rmsnorm(mem_ref[0], g_ref[...]).astype(BF16)
    d = k_ref.shape[2]
    k_ref[0] = _dot(u, w_ref[:, :d])
    v_ref[0] = _dot(u, w_ref[:, d:])


def _mem_kv(mem, g, w):
    b, n, d = mem.shape
    blk = pl.BlockSpec((1, n, d), lambda i: (i, 0, 0))
    return pl.pallas_call(
        _mem_kv_body,
        grid=(b,),
        in_specs=[blk, _resident(g.shape), _resident(w.shape)],
        out_specs=[blk, blk],
        out_shape=[jax.ShapeDtypeStruct((b, n, d), F32)] * 2,
        compiler_params=_params(("parallel",)),
        name="mem_kv",
    )(mem, g, w)


def _mix_out_body(seq, x_ref, og_ref, of_ref, mk_ref, mv_ref, pre_ref, w2_ref, gn_ref, wgo_ref, wfo_ref, wmo_ref,
                  wout_ref, post_ref, o_ref, u_ref):
    x = x_ref[...]
    tm, d = x.shape
    u_ref[...] = _rmsnorm(x, pre_ref[...]).astype(BF16)

    def proj(j):
        return _dot(u_ref[...], w2_ref[:, j * d:(j + 1) * d])

    mq = proj(1)
    hd = d // MEM_HEADS
    scale = float(hd ** -0.5)
    per_batch = []
    for j in range(tm // seq):
        mk = mk_ref[j].astype(BF16)
        mv = mv_ref[j].astype(BF16)
        heads = []
        for hh in range(MEM_HEADS):
            sl = slice(hh * hd, (hh + 1) * hd)
            s = _dot_nt(mq[j * seq:(j + 1) * seq, sl].astype(BF16), mk[:, sl]) * scale
            e = jnp.exp(s - jnp.max(s, axis=1, keepdims=True))
            p = e / jnp.sum(e, axis=1, keepdims=True)
            heads.append(_dot(p.astype(BF16), mv[:, sl]))
        per_batch.append(jnp.concatenate(heads, axis=1))
    o_mem = per_batch[0] if len(per_batch) == 1 else jnp.concatenate(per_batch, axis=0)
    y = _sigmoid(proj(4)) * _dot(o_mem.astype(BF16), wmo_ref[...])

    y = y + _sigmoid(proj(3)) * _dot(of_ref[...], wfo_ref[...])

    og = og_ref[...].astype(F32)
    gd = d // GLA_HEADS
    gn = gn_ref[...]
    normed = []
    for hh in range(GLA_HEADS):
        sl = slice(hh * gd, (hh + 1) * gd)
        normed.append(_rmsnorm(og[:, sl], gn[:, sl]))
    r = proj(0)
    b_gla = _dot((jnp.concatenate(normed, axis=1) * (r * _sigmoid(r))).astype(BF16), wgo_ref[...])
    y = y + _sigmoid(proj(2)) * b_gla

    mixed = _dot(y.astype(BF16), wout_ref[...])
    o_ref[...] = x + _rmsnorm(mixed, post_ref[...])


def _mix_out(x2, o_gla, o_fox, mk, mv, seq, pre_g, w2, gn, wgo, wfo, wmo, wout, post_g, tm=256):
    m, d = x2.shape
    if seq < tm:
        nb = 4
        while (m // seq) % nb:
            nb //= 2
        tm = seq * nb
    assert m % tm == 0 and (seq % tm == 0 or tm % seq == 0)
    nb = max(1, tm // seq)
    steps_per_batch = max(1, seq // tm)
    mem = mk.shape[1]
    rows = pl.BlockSpec((tm, d), lambda i: (i, 0))
    memb = pl.BlockSpec((nb, mem, d), lambda i: (i // steps_per_batch, 0, 0))
    return pl.pallas_call(
        functools.partial(_mix_out_body, min(seq, tm)),
        grid=(m // tm,),
        in_specs=[rows, rows, rows, memb, memb, _resident(pre_g.shape), _resident(w2.shape), _resident(gn.shape),
                  _resident(wgo.shape), _resident(wfo.shape), _resident(wmo.shape), _resident(wout.shape),
                  _resident(post_g.shape)],
        out_specs=rows,
        out_shape=jax.ShapeDtypeStruct((m, d), F32),
        scratch_shapes=[pltpu.VMEM((tm, d), BF16)],
        compiler_params=_params(("parallel",)),
        name="mix_out",
    )(x2, o_gla, o_fox, mk, mv, pre_g, w2, gn, wgo, wfo, wmo, wout, post_g)


def _prep_layer(d, w):
    gk_w = d // 2
    sizes = (gk_w, gk_w, d, GLA_RANK, d, d, d, d, FOX_HEADS, d, N_BRANCH * d)
    offs = np.concatenate([[0], np.cumsum(sizes)]).tolist()
    seg = {n: (offs[i], offs[i + 1]) for i, n in enumerate(
        ("gq", "gk", "gv", "glr", "gr", "fq", "fk", "fv", "ff", "mq", "gates"))}
    w_in = w["w_in"]

    def cols(*names):
        return jnp.concatenate([w_in[:, seg[n][0]:seg[n][1]] for n in names], axis=1).astype(BF16)

    ws = jnp.pad(cols("ff", "glr"), ((0, 0), (0, SMALL_W - FOX_HEADS - GLA_RANK)))
    wa2p = jnp.pad(w["gla_w_a2"].astype(BF16), ((FOX_HEADS, SMALL_W - FOX_HEADS - GLA_RANK), (0, 0)))
    b_f = jnp.pad(w["fox_b_f"], (0, SMALL_W - FOX_HEADS)).reshape(1, SMALL_W)

    def ffn(pfx):
        wg, wu, wd = w[pfx + "_w_gate"], w[pfx + "_w_up"], w[pfx + "_w_down"]
        nc = wg.shape[1] // FFN_CHUNK
        return dict(pre=w[pfx + "_pre_g"].reshape(1, d), post=w[pfx + "_post_g"].reshape(1, d),
                    wg=wg.astype(BF16).reshape(d, nc, FFN_CHUNK).transpose(1, 0, 2),
                    wu=wu.astype(BF16).reshape(d, nc, FFN_CHUNK).transpose(1, 0, 2),
                    wd=wd.astype(BF16).reshape(nc, FFN_CHUNK, d))

    return dict(
        ffn1=ffn("ffn1"), ffn2=ffn("ffn2"),
        mix_pre=w["mix_pre_g"].reshape(1, d), mix_post=w["mix_post_g"].reshape(1, d),
        w1=cols("gq", "gk", "gv", "fq", "fk", "fv"), ws=ws, wa2p=wa2p,
        b_a=w["gla_b_a"].reshape(1, gk_w), b_f=b_f,
        w2=cols("gr", "mq", "gates"),
        gn=w["gla_norm_g"].reshape(1, d),
        wgo=w["w_gla_o"].astype(BF16), wfo=w["w_fox_o"].astype(BF16), wmo=w["w_mem_o"].astype(BF16),
        wout=w["w_out"].astype(BF16),
        mem_g=w["mem_norm_g"].reshape(1, d), w_mem_kv=w["w_mem_kv"].astype(BF16),
    )


def _ffn_apply(x2, p):
    return _ffn(x2, p["pre"], p["post"], p["wg"], p["wu"], p["wd"])


def _forget_layouts(fl2, b, t, lead=None):
    f = fl2.reshape(b, t, SMALL_W)[:, :, :FOX_HEADS]
    if lead is not None:
        f = jnp.concatenate([lead.astype(F32), f], axis=1)
    total = f.shape[1]
    padded = -(-total // LANES) * LANES
    f = jnp.pad(f, ((0, 0), (0, padded - total), (0, 0)))
    rows = _cumsum_time(jnp.swapaxes(f, 1, 2).reshape(b * FOX_HEADS, padded // LANES, LANES))
    rows = rows.reshape(b, FOX_HEADS, padded)
    cols = jnp.swapaxes(rows[:, :, total - t:total], 1, 2)
    return cols, rows


def _layer_prompt(x2, b, t, mem, p):
    d = x2.shape[1]
    x2 = _ffn_apply(x2, p["ffn1"])
    gq, gk, gv, la, fq, fk32, fv32, fk16, fv16, fl = _mix_in(x2, p["mix_pre"], p["w1"], p["ws"], p["wa2p"],
                                                              p["b_a"], p["b_f"])
    dk = d // 2 // GLA_HEADS
    s0 = jnp.zeros((b, GLA_HEADS, dk, d // GLA_HEADS), F32)
    r3 = lambda a: a.reshape(b, t, a.shape[1])
    o_gla, s_fin = _gla(r3(gq), r3(gk), r3(gv), r3(la), s0)
    tq = min(512, t)
    f_cols, f_rows = _forget_layouts(fl, b, t)
    o_fox = _fox_prompt(r3(fq), r3(fk16), r3(fv16), f_cols, f_rows.reshape(b, FOX_HEADS, t // tq, tq), tq=tq)
    mk, mv = _mem_kv(mem, p["mem_g"], p["w_mem_kv"])
    x2 = _mix_out(x2, o_gla.reshape(b * t, d), o_fox.reshape(b * t, d), mk, mv, t, p["mix_pre"], p["w2"], p["gn"],
                  p["wgo"], p["wfo"], p["wmo"], p["wout"], p["mix_post"])
    x2 = _ffn_apply(x2, p["ffn2"])
    hd = d // FOX_HEADS
    f_log = fl.reshape(b, t, SMALL_W)[:, :, :FOX_HEADS]
    mh = d // MEM_HEADS
    outs = (fk32.reshape(b, t, FOX_HEADS, hd), fv32.reshape(b, t, FOX_HEADS, hd), f_log, s_fin,
            mk.reshape(b, -1, MEM_HEADS, mh), mv.reshape(b, -1, MEM_HEADS, mh))
    return x2, outs


def _layer_sample(x2, b, t, cache_k, cache_v, cache_f, state, mem_k, mem_v, p):
    d = x2.shape[1]
    x2 = _ffn_apply(x2, p["ffn1"])
    gq, gk, gv, la, fq, fk32, fv32, fk16, fv16, fl = _mix_in(x2, p["mix_pre"], p["w1"], p["ws"], p["wa2p"],
                                                              p["b_a"], p["b_f"])
    r3 = lambda a: a.reshape(b, t, a.shape[1])
    o_gla, s_new = _gla(r3(gq), r3(gk), r3(gv), r3(la), state)
    past = cache_k.shape[1]
    f_cols, f_rows = _forget_layouts(fl, b, t, lead=cache_f)
    o_fox = _fox_sample(r3(fq), r3(fk16), r3(fv16), cache_k.reshape(b, past, d), cache_v.reshape(b, past, d),
                        f_cols, f_rows[:, :, None, :])
    mem = mem_k.shape[1]
    x2 = _mix_out(x2, o_gla.reshape(b * t, d), o_fox.reshape(b * t, d), mem_k.reshape(b, mem, d),
                  mem_v.reshape(b, mem, d), t, p["mix_pre"], p["w2"], p["gn"], p["wgo"], p["wfo"], p["wmo"],
                  p["wout"], p["mix_post"])
    x2 = _ffn_apply(x2, p["ffn2"])
    hd = d // FOX_HEADS
    f_log = fl.reshape(b, t, SMALL_W)[:, :, :FOX_HEADS]
    return x2, (fk32.reshape(b, t, FOX_HEADS, hd), fv32.reshape(b, t, FOX_HEADS, hd), f_log, s_new)


def kernel(x_prompt, x_sample, cache_fox_k, cache_fox_v, cache_fox_logf, state_gla, cache_mem_k, cache_mem_v, mem_prompt, ffn1_pre_g, ffn1_post_g, ffn1_w_gate, ffn1_w_up, ffn1_w_down, mix_pre_g, mix_post_g, w_in, gla_w_a2, gla_b_a, fox_b_f, gla_norm_g, w_gla_o, w_fox_o, w_mem_o, w_out, mem_norm_g, w_mem_kv, ffn2_pre_g, ffn2_post_g, ffn2_w_gate, ffn2_w_up, ffn2_w_down):
    weights = dict(ffn1_pre_g=ffn1_pre_g, ffn1_post_g=ffn1_post_g, ffn1_w_gate=ffn1_w_gate, ffn1_w_up=ffn1_w_up,
                   ffn1_w_down=ffn1_w_down, mix_pre_g=mix_pre_g, mix_post_g=mix_post_g, w_in=w_in,
                   gla_w_a2=gla_w_a2, gla_b_a=gla_b_a, fox_b_f=fox_b_f, gla_norm_g=gla_norm_g, w_gla_o=w_gla_o,
                   w_fox_o=w_fox_o, w_mem_o=w_mem_o, w_out=w_out, mem_norm_g=mem_norm_g, w_mem_kv=w_mem_kv,
                   ffn2_pre_g=ffn2_pre_g, ffn2_post_g=ffn2_post_g, ffn2_w_gate=ffn2_w_gate, ffn2_w_up=ffn2_w_up,
                   ffn2_w_down=ffn2_w_down)
    depth = w_in.shape[0]
    bp, tp, d = x_prompt.shape
    bs, ts, _ = x_sample.shape
    layers = [_prep_layer(d, {k: v[l] for k, v in weights.items()}) for l in range(depth)]

    x = x_prompt.reshape(bp * tp, d)
    p_outs = []
    for l in range(depth):
        x, outs = _layer_prompt(x, bp, tp, mem_prompt, layers[l])
        p_outs.append(outs)
    y_prompt = x.reshape(bp, tp, d)

    x = x_sample.reshape(bs * ts, d)
    s_outs = []
    for l in range(depth):
        x, outs = _layer_sample(x, bs, ts, cache_fox_k[l], cache_fox_v[l], cache_fox_logf[l], state_gla[l],
                                cache_mem_k[l], cache_mem_v[l], layers[l])
        s_outs.append(outs)
    y_sample = x.reshape(bs, ts, d)

    stack = lambda outs, i: jnp.stack([o[i] for o in outs])
    return (y_prompt, y_sample) + tuple(stack(p_outs, i) for i in range(6)) + tuple(stack(s_outs, i) for i in range(4))
```

```python
import functools

import numpy as np
import jax
import jax.numpy as jnp
from jax import lax
from jax.experimental import pallas as pl
from jax.experimental.pallas import tpu as pltpu

F32 = jnp.float32
BF16 = jnp.bfloat16

EPS = 1e-6
GLA_HEADS = 4
GLA_RANK = 16
GLA_TAU = 16.0
FOX_HEADS = 8
MEM_HEADS = 4
N_BRANCH = 3

LANES = 128
V7X_VMEM_LIMIT = 56 * 1024 * 1024
FFN_CHUNK = 256
GLA_CHUNK = 64
GLA_SUB = 16
FOX_HEADS_PER_STEP = 4
LOG2E = 1.4426950408889634
SMALL_W = 128


def _dot(a, b):
    return jnp.dot(a, b, preferred_element_type=F32)


def _dot_nt(a, b):
    return lax.dot_general(a, b, (((1,), (1,)), ((), ())), preferred_element_type=F32)


def _dot_tn(a, b):
    return lax.dot_general(a, b, (((0,), (0,)), ((), ())), preferred_element_type=F32)


def _rmsnorm(x, g):
    return x * lax.rsqrt(jnp.mean(x * x, axis=-1, keepdims=True) + EPS) * g


def _sigmoid(x):
    return 1.0 / (1.0 + jnp.exp(-x))


def _log_sigmoid(x):
    return jnp.minimum(x, 0.0) - jnp.log1p(jnp.exp(-jnp.abs(x)))


def _split3(x):
    hi = x.astype(BF16)
    r1 = x - hi.astype(F32)
    mid = r1.astype(BF16)
    lo = (r1 - mid.astype(F32)).astype(BF16)
    return hi, mid, lo


def _resident(shape):
    nd = len(shape)
    return pl.BlockSpec(shape, lambda *_: (0,) * nd, pipeline_mode=pl.Buffered(1))


def _params(semantics):
    return pltpu.CompilerParams(dimension_semantics=semantics, vmem_limit_bytes=V7X_VMEM_LIMIT)


def _row_tile(m, want):
    t = min(m, want)
    assert m % t == 0, (m, t)
    return t


def _ffn_body(x_ref, pre_ref, post_ref, wg_ref, wu_ref, wd_ref, o_ref, u_ref, acc_ref):
    x = x_ref[...]
    u_ref[...] = _rmsnorm(x, pre_ref[...]).astype(BF16)
    for c in range(wg_ref.shape[0]):
        u = u_ref[...]
        g = _dot(u, wg_ref[c])
        up = _dot(u, wu_ref[c])
        h = (g * _sigmoid(g) * up).astype(BF16)
        part = _dot(h, wd_ref[c])
        if c == 0:
            acc_ref[...] = part
        else:
            acc_ref[...] += part
    o_ref[...] = x + 0.5 * _rmsnorm(acc_ref[...], post_ref[...])


def _ffn(x2, pre_g, post_g, wg, wu, wd, tm=512):
    m, d = x2.shape
    tm = _row_tile(m, tm)
    return pl.pallas_call(
        _ffn_body,
        grid=(m // tm,),
        in_specs=[pl.BlockSpec((tm, d), lambda i: (i, 0)),
                  _resident(pre_g.shape), _resident(post_g.shape),
                  _resident(wg.shape), _resident(wu.shape), _resident(wd.shape)],
        out_specs=pl.BlockSpec((tm, d), lambda i: (i, 0)),
        out_shape=jax.ShapeDtypeStruct((m, d), F32),
        scratch_shapes=[pltpu.VMEM((tm, d), BF16), pltpu.VMEM((tm, d), F32)],
        compiler_params=_params(("parallel",)),
        name="ffn",
    )(x2, pre_g, post_g, wg, wu, wd)


def _mix_in_body(dims, n_carried, x_ref, pre_ref, w_ref, ws_ref, wa2_ref, ba_ref, bf_ref, *refs):
    gq_ref, gk_ref, gv_ref, la_ref, fq_ref, fk16_ref, fv16_ref, fk32_ref, fv32_ref, fl_ref, u_ref = refs[n_carried:]
    gk_w, gv_w, fx_w, dk_scale, hd_scale = dims
    u_ref[...] = _rmsnorm(x_ref[...], pre_ref[...]).astype(BF16)

    def proj(lo, width):
        return _dot(u_ref[...], w_ref[:, lo:lo + width])

    off = 0
    gq_ref[...] = proj(off, gk_w) * dk_scale
    off += gk_w
    gk_ref[...] = proj(off, gk_w)
    off += gk_w
    gv_ref[...] = proj(off, gv_w).astype(BF16)
    off += gv_w
    fq_ref[...] = (proj(off, fx_w) * hd_scale).astype(BF16)
    off += fx_w
    fk = proj(off, fx_w)
    fk32_ref[...] = fk
    fk16_ref[...] = fk.astype(BF16)
    off += fx_w
    fv = proj(off, fx_w)
    fv32_ref[...] = fv
    fv16_ref[...] = fv.astype(BF16)

    small = _dot(u_ref[...], ws_ref[...])
    lane = lax.broadcasted_iota(jnp.int32, small.shape, 1)
    fl_ref[...] = jnp.where(lane < FOX_HEADS, _log_sigmoid(small + bf_ref[...]), 0.0)
    lr = jnp.where((lane >= FOX_HEADS) & (lane < FOX_HEADS + GLA_RANK), small, 0.0).astype(BF16)
    la_ref[...] = _log_sigmoid(_dot(lr, wa2_ref[...]) + ba_ref[...]) * (1.0 / GLA_TAU)


def _mix_in(x2, pre_g, w1, ws, wa2p, b_a, b_f, layer, depth, carried, tm=512):
    m, d = x2.shape
    tm = _row_tile(m, tm)
    gk_w = d // 2
    dims = (gk_w, d, d, float((gk_w // GLA_HEADS) ** -0.5), float((d // FOX_HEADS) ** -0.5) * LOG2E)

    def rows(width):
        return pl.BlockSpec((tm, width), lambda i: (i, 0))

    def stacked(width):
        return pl.BlockSpec((None, tm, width), lambda i: (layer, i, 0))

    outs = [(gk_w, F32), (gk_w, F32), (d, BF16), (gk_w, F32), (d, BF16), (d, BF16), (d, BF16)]
    stacks = [d, d, SMALL_W]
    n_in = 7
    return pl.pallas_call(
        functools.partial(_mix_in_body, dims, len(carried)),
        grid=(m // tm,),
        in_specs=[rows(d), _resident(pre_g.shape), _resident(w1.shape), _resident(ws.shape),
                  _resident(wa2p.shape), _resident(b_a.shape), _resident(b_f.shape)]
                 + [pl.BlockSpec(memory_space=pl.ANY)] * len(carried),
        out_specs=[rows(w) for w, _ in outs] + [stacked(w) for w in stacks],
        out_shape=[jax.ShapeDtypeStruct((m, w), dt) for w, dt in outs]
                  + [jax.ShapeDtypeStruct((depth, m, w), F32) for w in stacks],
        input_output_aliases={n_in + j: len(outs) + j for j in range(len(carried))},
        scratch_shapes=[pltpu.VMEM((tm, d), BF16)],
        compiler_params=_params(("parallel",)),
        name="mix_in",
    )(x2, pre_g, w1, ws, wa2p, b_a, b_f, *carried)


def _cumsum_body(x_ref, m_ref, o_ref):
    x = x_ref[...]
    r = lax.broadcasted_iota(jnp.int32, (LANES, LANES), 0)
    c = lax.broadcasted_iota(jnp.int32, (LANES, LANES), 1)
    upper = jnp.where(r <= c, 1.0, 0.0).astype(BF16)
    hi, mid, lo = _split3(x)
    o_ref[...] = _dot(hi, upper) + _dot(mid, upper) + _dot(lo, upper)
    tot = jnp.broadcast_to(o_ref[:, LANES - 1:LANES], x.shape)
    hi, mid, lo = _split3(tot)
    mm = m_ref[...]
    offset = _dot(mm, hi) + _dot(mm, mid) + _dot(mm, lo)
    o_ref[...] = o_ref[...] + offset


def _cumsum_time(x3):
    g, n, _ = x3.shape
    per = 8
    while (per * n) % 8:
        per += 1
    gpb = per if g % per == 0 else g
    rb = gpb * n
    idx = np.arange(rb)
    earlier = (idx[:, None] // n == idx[None, :] // n) & (idx[None, :] < idx[:, None])
    mm = jnp.asarray(earlier.astype(np.float32), BF16)
    out = pl.pallas_call(
        _cumsum_body,
        grid=(g // gpb,),
        in_specs=[pl.BlockSpec((rb, LANES), lambda i: (i, 0)), _resident((rb, rb))],
        out_specs=pl.BlockSpec((rb, LANES), lambda i: (i, 0)),
        out_shape=jax.ShapeDtypeStruct((g * n, LANES), F32),
        compiler_params=_params(("parallel",)),
        name="cumsum",
    )(x3.reshape(g * n, LANES), mm)
    return out.reshape(g, n, LANES)


def _gla_chunk(q, k, v, la, st, place_ref):
    cl, dk = q.shape
    sub = min(GLA_SUB, cl)
    row = lax.broadcasted_iota(jnp.int32, (cl, cl), 0)
    col = lax.broadcasted_iota(jnp.int32, (cl, cl), 1)
    lower = jnp.where(row >= col, 1.0, 0.0).astype(BF16)
    hi, mid, lo = _split3(la)
    b = _dot(lower, hi) + _dot(lower, mid) + _dot(lower, lo)
    b_end = b[cl - 1:cl, :]

    o = _dot_nt((q * jnp.exp(b)).astype(BF16), st.astype(BF16))

    tok = lax.broadcasted_iota(jnp.int32, (cl, dk), 0)
    tsub = lax.broadcasted_iota(jnp.int32, (sub, dk), 0)
    blocks = []
    for i in range(cl // sub):
        r0 = i * sub
        b_i, q_i, k_i = b[r0:r0 + sub], q[r0:r0 + sub], k[r0:r0 + sub]
        cols = []
        for s in range(sub):
            rel = jnp.where(tsub >= s, b_i - b_i[s:s + 1], -jnp.inf)
            cols.append((q_i * k_i[s:s + 1] * jnp.exp(rel)).astype(BF16))
        a_i = _dot(jnp.concatenate(cols, axis=1), place_ref[i])
        if i > 0:
            ref = b[r0 - 1:r0]
            q_t = (q_i * jnp.exp(b_i - ref)).astype(BF16)
            k_t = (k * jnp.exp(jnp.where(tok < r0, ref - b, -jnp.inf))).astype(BF16)
            a_i = a_i + _dot_nt(q_t, k_t)
        blocks.append(a_i)
    attn = jnp.concatenate(blocks, axis=0).astype(BF16)
    o = o + _dot(attn, v)

    k_d = (k * jnp.exp(b_end - b)).astype(BF16)
    st_new = st * jnp.exp(b_end) + _dot_tn(v, k_d)
    return o, st_new


def _gla_body(cl, q_ref, k_ref, v_ref, la_ref, s0_ref, place_ref, o_ref, sf_ref, st_ref):
    t = pl.program_id(1)
    heads, dv, dk = st_ref.shape

    @pl.when(t == 0)
    def _():
        for hh in range(heads):
            st_ref[hh] = s0_ref[0, hh].T

    n_chunks = q_ref.shape[1] // cl

    def chunk(ci, carry):
        r = pl.multiple_of(ci * cl, cl)
        for hh in range(heads):
            ks = slice(hh * dk, (hh + 1) * dk)
            vs = slice(hh * dv, (hh + 1) * dv)
            o, st = _gla_chunk(q_ref[0, pl.ds(r, cl), ks], k_ref[0, pl.ds(r, cl), ks], v_ref[0, pl.ds(r, cl), vs],
                               la_ref[0, pl.ds(r, cl), ks], st_ref[hh], place_ref)
            o_ref[0, pl.ds(r, cl), vs] = o.astype(o_ref.dtype)
            st_ref[hh] = st
        return carry

    lax.fori_loop(0, n_chunks, chunk, 0, unroll=2 if n_chunks % 2 == 0 else 1)

    @pl.when(t == pl.num_programs(1) - 1)
    def _():
        for hh in range(heads):
            sf_ref[0, hh] = st_ref[hh].T


def _gla_place(cl, dk):
    sub = min(GLA_SUB, cl)
    place = np.zeros((cl // sub, sub * dk, cl), np.float32)
    for i in range(cl // sub):
        for s in range(sub):
            place[i, s * dk:(s + 1) * dk, i * sub + s] = 1.0
    return jnp.asarray(place, BF16)


def _gla(gq, gk, gv, la, s0, s0_off=0, tt=512):
    b, t, _ = gq.shape
    _, h, dk, dv = s0.shape
    cl = min(GLA_CHUNK, t)
    tt = _row_tile(t, tt)
    place = _gla_place(cl, dk)
    return pl.pallas_call(
        functools.partial(_gla_body, cl),
        grid=(b, t // tt),
        in_specs=[pl.BlockSpec((1, tt, h * dk), lambda i, n: (i, n, 0)),
                  pl.BlockSpec((1, tt, h * dk), lambda i, n: (i, n, 0)),
                  pl.BlockSpec((1, tt, h * dv), lambda i, n: (i, n, 0)),
                  pl.BlockSpec((1, tt, h * dk), lambda i, n: (i, n, 0)),
                  pl.BlockSpec((1, h, dk, dv), lambda i, n: (i + s0_off, 0, 0, 0)),
                  _resident(place.shape)],
        out_specs=[pl.BlockSpec((1, tt, h * dv), lambda i, n: (i, n, 0)),
                   pl.BlockSpec((1, h, dk, dv), lambda i, n: (i, 0, 0, 0))],
        out_shape=[jax.ShapeDtypeStruct((b, t, h * dv), BF16), jax.ShapeDtypeStruct((b, h, dk, dv), F32)],
        scratch_shapes=[pltpu.VMEM((h, dv, dk), F32)],
        compiler_params=_params(("parallel", "arbitrary")),
        name="gla",
    )(gq, gk, gv, la, s0, place)


def _head_column(block, head):
    lane = lax.broadcasted_iota(jnp.int32, block.shape, 1)
    return jnp.sum(jnp.where(lane == head, block, 0.0), axis=1, keepdims=True)


def _fox_prompt_body(hps, q_ref, k_ref, v_ref, fq_ref, fk_ref, o_ref, m_ref, acc_ref):
    group = pl.program_id(1)
    qi = pl.program_id(2)
    tq = q_ref.shape[1]
    hd = q_ref.shape[2] // hps
    lane = lax.broadcasted_iota(jnp.int32, (tq, LANES), 1)
    ones_col = jnp.where(lane == 0, 1.0, 0.0).astype(BF16)
    f_q = [_head_column(fq_ref[0], group * hps + hh) * LOG2E for hh in range(hps)]

    def logits(hh, kb, r):
        sl = slice(hh * hd, (hh + 1) * hd)
        s = _dot_nt(q_ref[0, :, sl], k_ref[0, pl.ds(r, tq), sl])
        return s + f_q[hh] - fk_ref[0, hh, pl.ds(kb, 1), :] * LOG2E

    def values(hh, r):
        return jnp.concatenate([v_ref[0, pl.ds(r, tq), hh * hd:(hh + 1) * hd], ones_col], axis=1)

    r = pl.multiple_of(qi * tq, tq)
    row = lax.broadcasted_iota(jnp.int32, (tq, tq), 0)
    col = lax.broadcasted_iota(jnp.int32, (tq, tq), 1)
    for hh in range(hps):
        z = jnp.where(col <= row, logits(hh, qi, r), -jnp.inf)
        m = jnp.max(z, axis=1, keepdims=True)
        m_ref[hh] = m
        acc_ref[hh] = _dot(jnp.exp2(z - m).astype(BF16), values(hh, r))

    def block(kb, carry):
        r = pl.multiple_of(kb * tq, tq)
        for hh in range(hps):
            z = logits(hh, kb, r)
            m_old = m_ref[hh]
            m_new = jnp.maximum(m_old, jnp.max(z, axis=1, keepdims=True))
            m_ref[hh] = m_new
            pv = _dot(jnp.exp2(z - m_new).astype(BF16), values(hh, r))
            acc_ref[hh] = jnp.exp2(m_old - m_new) * acc_ref[hh] + pv
        return carry

    lax.fori_loop(0, qi, block, 0)
    for hh in range(hps):
        acc = acc_ref[hh]
        o_ref[0, :, hh * hd:(hh + 1) * hd] = (acc[:, :hd] / acc[:, hd:hd + 1]).astype(o_ref.dtype)


def _fox_prompt(fq, fk, fv, f_cols, f_rows, tq=512, hps=FOX_HEADS_PER_STEP):
    b, t, d = fq.shape
    h = f_cols.shape[2]
    hd = d // h
    nq = t // tq
    return pl.pallas_call(
        functools.partial(_fox_prompt_body, hps),
        grid=(b, h // hps, nq),
        in_specs=[pl.BlockSpec((1, tq, hps * hd), lambda i, j, n: (i, n, j)),
                  pl.BlockSpec((1, t, hps * hd), lambda i, j, n: (i, 0, j)),
                  pl.BlockSpec((1, t, hps * hd), lambda i, j, n: (i, 0, j)),
                  pl.BlockSpec((1, tq, h), lambda i, j, n: (i, n, 0)),
                  pl.BlockSpec((1, hps, nq, tq), lambda i, j, n: (i, j, 0, 0))],
        out_specs=pl.BlockSpec((1, tq, hps * hd), lambda i, j, n: (i, n, j)),
        out_shape=jax.ShapeDtypeStruct((b, t, d), BF16),
        scratch_shapes=[pltpu.VMEM((hps, tq, 1), F32), pltpu.VMEM((hps, tq, hd + LANES), F32)],
        compiler_params=_params(("parallel", "parallel", "arbitrary")),
        name="fox_prompt",
    )(fq, fk, fv, f_cols, f_rows)


def _fox_sample_body(q_ref, kc_ref, vc_ref, kn_ref, vn_ref, fq_ref, fk_ref, o_ref):
    head = pl.program_id(1)
    q = q_ref[0]
    ln = q.shape[0]
    past = kc_ref.shape[1]
    f_q = _head_column(fq_ref[0], head) * LOG2E
    f_k = fk_ref[0, 0] * LOG2E
    z_c = _dot_nt(q, kc_ref[0].astype(BF16)) + f_q - f_k[:, :past]
    z_n = _dot_nt(q, kn_ref[0]) + f_q - f_k[:, past:past + ln]
    row = lax.broadcasted_iota(jnp.int32, z_n.shape, 0)
    col = lax.broadcasted_iota(jnp.int32, z_n.shape, 1)
    z_n = jnp.where(col <= row, z_n, -jnp.inf)
    m = jnp.maximum(jnp.max(z_c, axis=1, keepdims=True), jnp.max(z_n, axis=1, keepdims=True))
    p_c = jnp.exp2(z_c - m)
    p_n = jnp.exp2(z_n - m)
    denom = jnp.sum(p_c, axis=1, keepdims=True) + jnp.sum(p_n, axis=1, keepdims=True)
    o = _dot(p_c.astype(BF16), vc_ref[0].astype(BF16)) + _dot(p_n.astype(BF16), vn_ref[0])
    o_ref[0] = (o / denom).astype(o_ref.dtype)


def _fox_sample(fq, fk, fv, cache_k, cache_v, cache_off, f_cols, f_rows):
    b, ln, d = fq.shape
    past = cache_k.shape[1]
    h = f_cols.shape[2]
    hd = d // h
    fw = f_rows.shape[3]
    return pl.pallas_call(
        _fox_sample_body,
        grid=(b, h),
        in_specs=[pl.BlockSpec((1, ln, hd), lambda i, j: (i, 0, j)),
                  pl.BlockSpec((1, past, hd), lambda i, j: (i + cache_off, 0, j)),
                  pl.BlockSpec((1, past, hd), lambda i, j: (i + cache_off, 0, j)),
                  pl.BlockSpec((1, ln, hd), lambda i, j: (i, 0, j)),
                  pl.BlockSpec((1, ln, hd), lambda i, j: (i, 0, j)),
                  pl.BlockSpec((1, ln, h), lambda i, j: (i, 0, 0)),
                  pl.BlockSpec((1, 1, 1, fw), lambda i, j: (i, j, 0, 0))],
        out_specs=pl.BlockSpec((1, ln, hd), lambda i, j: (i, 0, j)),
        out_shape=jax.ShapeDtypeStruct((b, ln, d), BF16),
        compiler_params=_params(("parallel", "parallel")),
        name="fox_sample",
    )(fq, cache_k, cache_v, fk, fv, f_cols, f_rows)


def _mem_kv_body(mem_ref, g_ref, w_ref, k_ref, v_ref):
    u = _rmsnorm(mem_ref[0], g_ref[...]).astype(BF16)
    d = k_ref.shape[2]
    k_ref[0] = _dot(u, w_ref[:, :d])
    v_ref[0] = _dot(u, w_ref[:, d:])


def _mem_kv(mem, g, w):
    b, n, d = mem.shape
    blk = pl.BlockSpec((1, n, d), lambda i: (i, 0, 0))
    return pl.pallas_call(
        _mem_kv_body,
        grid=(b,),
        in_specs=[blk, _resident(g.shape), _resident(w.shape)],
        out_specs=[blk, blk],
        out_shape=[jax.ShapeDtypeStruct((b, n, d), F32)] * 2,
        compiler_params=_params(("parallel",)),
        name="mem_kv",
    )(mem, g, w)


def _mix_out_body(seq, x_ref, og_ref, of_ref, mk_ref, mv_ref, pre_ref, w2_ref, gn_ref, wgo_ref, wfo_ref, wmo_ref,
                  wout_ref, post_ref, o_ref, u_ref):
    x = x_ref[...]
    tm, d = x.shape
    u_ref[...] = _rmsnorm(x, pre_ref[...]).astype(BF16)

    def proj(j):
        return _dot(u_ref[...], w2_ref[:, j * d:(j + 1) * d])

    mq = proj(1)
    hd = d // MEM_HEADS
    scale = float(hd ** -0.5)
    per_batch = []
    for j in range(tm // seq):
        mk = mk_ref[j].astype(BF16)
        mv = mv_ref[j].astype(BF16)
        heads = []
        for hh in range(MEM_HEADS):
            sl = slice(hh * hd, (hh + 1) * hd)
            s = _dot_nt(mq[j * seq:(j + 1) * seq, sl].astype(BF16), mk[:, sl]) * scale
            e = jnp.exp(s - jnp.max(s, axis=1, keepdims=True))
            p = e / jnp.sum(e, axis=1, keepdims=True)
            heads.append(_dot(p.astype(BF16), mv[:, sl]))
        per_batch.append(jnp.concatenate(heads, axis=1))
    o_mem = per_batch[0] if len(per_batch) == 1 else jnp.concatenate(per_batch, axis=0)
    y = _sigmoid(proj(4)) * _dot(o_mem.astype(BF16), wmo_ref[...])

    y = y + _sigmoid(proj(3)) * _dot(of_ref[...], wfo_ref[...])

    og = og_ref[...].astype(F32)
    gd = d // GLA_HEADS
    gn = gn_ref[...]
    normed = []
    for hh in range(GLA_HEADS):
        sl = slice(hh * gd, (hh + 1) * gd)
        normed.append(_rmsnorm(og[:, sl], gn[:, sl]))
    r = proj(0)
    b_gla = _dot((jnp.concatenate(normed, axis=1) * (r * _sigmoid(r))).astype(BF16), wgo_ref[...])
    y = y + _sigmoid(proj(2)) * b_gla

    mixed = _dot(y.astype(BF16), wout_ref[...])
    o_ref[...] = x + _rmsnorm(mixed, post_ref[...])


def _mix_out(x2, o_gla, o_fox, mk, mv, mem_off, seq, pre_g, w2, gn, wgo, wfo, wmo, wout, post_g, tm=256):
    m, d = x2.shape
    if seq < tm:
        nb = 4
        while (m // seq) % nb:
            nb //= 2
        tm = seq * nb
    assert m % tm == 0 and (seq % tm == 0 or tm % seq == 0)
    nb = max(1, tm // seq)
    steps_per_batch = max(1, seq // tm)
    mem = mk.shape[1]
    rows = pl.BlockSpec((tm, d), lambda i: (i, 0))
    assert mem_off % nb == 0
    memb = pl.BlockSpec((nb, mem, d), lambda i: (i // steps_per_batch + mem_off // nb, 0, 0))
    return pl.pallas_call(
        functools.partial(_mix_out_body, min(seq, tm)),
        grid=(m // tm,),
        in_specs=[rows, rows, rows, memb, memb, _resident(pre_g.shape), _resident(w2.shape), _resident(gn.shape),
                  _resident(wgo.shape), _resident(wfo.shape), _resident(wmo.shape), _resident(wout.shape),
                  _resident(post_g.shape)],
        out_specs=rows,
        out_shape=jax.ShapeDtypeStruct((m, d), F32),
        scratch_shapes=[pltpu.VMEM((tm, d), BF16)],
        compiler_params=_params(("parallel",)),
        name="mix_out",
    )(x2, o_gla, o_fox, mk, mv, pre_g, w2, gn, wgo, wfo, wmo, wout, post_g)


def _prep_layer(d, w):
    gk_w = d // 2
    sizes = (gk_w, gk_w, d, GLA_RANK, d, d, d, d, FOX_HEADS, d, N_BRANCH * d)
    offs = np.concatenate([[0], np.cumsum(sizes)]).tolist()
    seg = {n: (offs[i], offs[i + 1]) for i, n in enumerate(
        ("gq", "gk", "gv", "glr", "gr", "fq", "fk", "fv", "ff", "mq", "gates"))}
    w_in = w["w_in"]

    def cols(*names):
        return jnp.concatenate([w_in[:, seg[n][0]:seg[n][1]] for n in names], axis=1).astype(BF16)

    ws = jnp.pad(cols("ff", "glr"), ((0, 0), (0, SMALL_W - FOX_HEADS - GLA_RANK)))
    wa2p = jnp.pad(w["gla_w_a2"].astype(BF16), ((FOX_HEADS, SMALL_W - FOX_HEADS - GLA_RANK), (0, 0)))
    b_f = jnp.pad(w["fox_b_f"], (0, SMALL_W - FOX_HEADS)).reshape(1, SMALL_W)

    def ffn(pfx):
        wg, wu, wd = w[pfx + "_w_gate"], w[pfx + "_w_up"], w[pfx + "_w_down"]
        nc = wg.shape[1] // FFN_CHUNK
        return dict(pre=w[pfx + "_pre_g"].reshape(1, d), post=w[pfx + "_post_g"].reshape(1, d),
                    wg=wg.astype(BF16).reshape(d, nc, FFN_CHUNK).transpose(1, 0, 2),
                    wu=wu.astype(BF16).reshape(d, nc, FFN_CHUNK).transpose(1, 0, 2),
                    wd=wd.astype(BF16).reshape(nc, FFN_CHUNK, d))

    return dict(
        ffn1=ffn("ffn1"), ffn2=ffn("ffn2"),
        mix_pre=w["mix_pre_g"].reshape(1, d), mix_post=w["mix_post_g"].reshape(1, d),
        w1=cols("gq", "gk", "gv", "fq", "fk", "fv"), ws=ws, wa2p=wa2p,
        b_a=w["gla_b_a"].reshape(1, gk_w), b_f=b_f,
        w2=cols("gr", "mq", "gates"),
        gn=w["gla_norm_g"].reshape(1, d),
        wgo=w["w_gla_o"].astype(BF16), wfo=w["w_fox_o"].astype(BF16), wmo=w["w_mem_o"].astype(BF16),
        wout=w["w_out"].astype(BF16),
        mem_g=w["mem_norm_g"].reshape(1, d), w_mem_kv=w["w_mem_kv"].astype(BF16),
    )


def _ffn_apply(x2, p):
    return _ffn(x2, p["pre"], p["post"], p["wg"], p["wu"], p["wd"])


def _forget_layouts(f, lead=None):
    b, t, h = f.shape
    if lead is not None:
        f = jnp.concatenate([lead.astype(F32), f], axis=1)
    total = f.shape[1]
    padded = -(-total // LANES) * LANES
    f = jnp.pad(f, ((0, 0), (0, padded - total), (0, 0)))
    rows = _cumsum_time(jnp.swapaxes(f, 1, 2).reshape(b * h, padded // LANES, LANES))
    rows = rows.reshape(b, h, padded)
    cols = jnp.swapaxes(rows[:, :, total - t:total], 1, 2)
    return cols, rows


def _group_layer(x2, b, t, p, layer, depth, carried, gla_state, fox, memory):
    d = x2.shape[1]
    x2 = _ffn_apply(x2, p["ffn1"])
    gq, gk, gv, la, fq, fk16, fv16, *carried = _mix_in(x2, p["mix_pre"], p["w1"], p["ws"], p["wa2p"], p["b_a"],
                                                       p["b_f"], layer, depth, carried)
    r3 = lambda a: a.reshape(b, t, a.shape[1])
    o_gla, s_fin = _gla(r3(gq), r3(gk), r3(gv), r3(la), gla_state[0], gla_state[1])
    f_log = carried[2][layer].reshape(b, t, SMALL_W)[:, :, :FOX_HEADS]
    o_fox = fox(r3(fq), r3(fk16), r3(fv16), f_log)
    x2 = _mix_out(x2, o_gla.reshape(b * t, d), o_fox.reshape(b * t, d), memory[0], memory[1], memory[2], t,
                  p["mix_pre"], p["w2"], p["gn"], p["wgo"], p["wfo"], p["wmo"], p["wout"], p["mix_post"])
    x2 = _ffn_apply(x2, p["ffn2"])
    return x2, carried, s_fin


def kernel(x_prompt, x_sample, cache_fox_k, cache_fox_v, cache_fox_logf, state_gla, cache_mem_k, cache_mem_v, mem_prompt, ffn1_pre_g, ffn1_post_g, ffn1_w_gate, ffn1_w_up, ffn1_w_down, mix_pre_g, mix_post_g, w_in, gla_w_a2, gla_b_a, fox_b_f, gla_norm_g, w_gla_o, w_fox_o, w_mem_o, w_out, mem_norm_g, w_mem_kv, ffn2_pre_g, ffn2_post_g, ffn2_w_gate, ffn2_w_up, ffn2_w_down):
    weights = dict(ffn1_pre_g=ffn1_pre_g, ffn1_post_g=ffn1_post_g, ffn1_w_gate=ffn1_w_gate, ffn1_w_up=ffn1_w_up,
                   ffn1_w_down=ffn1_w_down, mix_pre_g=mix_pre_g, mix_post_g=mix_post_g, w_in=w_in,
                   gla_w_a2=gla_w_a2, gla_b_a=gla_b_a, fox_b_f=fox_b_f, gla_norm_g=gla_norm_g, w_gla_o=w_gla_o,
                   w_fox_o=w_fox_o, w_mem_o=w_mem_o, w_out=w_out, mem_norm_g=mem_norm_g, w_mem_kv=w_mem_kv,
                   ffn2_pre_g=ffn2_pre_g, ffn2_post_g=ffn2_post_g, ffn2_w_gate=ffn2_w_gate, ffn2_w_up=ffn2_w_up,
                   ffn2_w_down=ffn2_w_down)
    depth = w_in.shape[0]
    bp, tp, d = x_prompt.shape
    bs, ts, _ = x_sample.shape
    hd = d // FOX_HEADS
    mh = d // MEM_HEADS
    dk, dv = state_gla.shape[-2:]
    layers = [_prep_layer(d, {k: v[l] for k, v in weights.items()}) for l in range(depth)]

    x = x_prompt.reshape(bp * tp, d)
    zero_state = jnp.zeros((bp, GLA_HEADS, dk, dv), F32)
    tq = min(512, tp)

    def fox_p(fq, fk, fv, f_log):
        f_cols, f_rows = _forget_layouts(f_log)
        return _fox_prompt(fq, fk, fv, f_cols, f_rows.reshape(bp, FOX_HEADS, tp // tq, tq), tq=tq)

    carried, p_state, p_mk, p_mv = [], [], [], []
    for l in range(depth):
        mk, mv = _mem_kv(mem_prompt, layers[l]["mem_g"], layers[l]["w_mem_kv"])
        x, carried, s_fin = _group_layer(x, bp, tp, layers[l], l, depth, carried, (zero_state, 0), fox_p, (mk, mv, 0))
        p_state.append(s_fin)
        p_mk.append(mk)
        p_mv.append(mv)
    y_prompt = x.reshape(bp, tp, d)
    p_fox_k = carried[0].reshape(depth, bp, tp, FOX_HEADS, hd)
    p_fox_v = carried[1].reshape(depth, bp, tp, FOX_HEADS, hd)
    p_fox_logf = carried[2].reshape(depth, bp, tp, SMALL_W)[..., :FOX_HEADS]
    mem_len = mem_prompt.shape[1]
    p_mem_k = jnp.stack(p_mk).reshape(depth, bp, mem_len, MEM_HEADS, mh)
    p_mem_v = jnp.stack(p_mv).reshape(depth, bp, mem_len, MEM_HEADS, mh)

    x = x_sample.reshape(bs * ts, d)
    past = cache_fox_k.shape[2]
    ck = cache_fox_k.reshape(depth * bs, past, d)
    cv = cache_fox_v.reshape(depth * bs, past, d)
    states = state_gla.reshape(depth * bs, GLA_HEADS, dk, dv)
    mem_s = cache_mem_k.shape[2]
    cmk = cache_mem_k.reshape(depth * bs, mem_s, d)
    cmv = cache_mem_v.reshape(depth * bs, mem_s, d)
    carried, s_state = [], []
    for l in range(depth):
        def fox_s(fq, fk, fv, f_log, l=l):
            f_cols, f_rows = _forget_layouts(f_log, lead=cache_fox_logf[l])
            return _fox_sample(fq, fk, fv, ck, cv, l * bs, f_cols, f_rows[:, :, None, :])

        x, carried, s_new = _group_layer(x, bs, ts, layers[l], l, depth, carried, (states, l * bs), fox_s,
                                         (cmk, cmv, l * bs))
        s_state.append(s_new)
    y_sample = x.reshape(bs, ts, d)
    s_fox_k = carried[0].reshape(depth, bs, ts, FOX_HEADS, hd)
    s_fox_v = carried[1].reshape(depth, bs, ts, FOX_HEADS, hd)
    s_fox_logf = carried[2].reshape(depth, bs, ts, SMALL_W)[..., :FOX_HEADS]

    return (y_prompt, y_sample, p_fox_k, p_fox_v, p_fox_logf, jnp.stack(p_state), p_mem_k, p_mem_v,
            s_fox_k, s_fox_v, s_fox_logf, jnp.stack(s_state))
```

```python
import functools

import numpy as np
import jax
import jax.numpy as jnp
from jax import lax
from jax.experimental import pallas as pl
from jax.experimental.pallas import tpu as pltpu

F32 = jnp.float32
BF16 = jnp.bfloat16

EPS = 1e-6
GLA_HEADS = 4
GLA_RANK = 16
GLA_TAU = 16.0
FOX_HEADS = 8
MEM_HEADS = 4
N_BRANCH = 3

LANES = 128
SUBLANES = 8
V7X_VMEM_LIMIT = 56 * 1024 * 1024
FFN_CHUNK = 256
GLA_CHUNK = 64
GLA_SUB = 16
FOX_HEADS_PER_STEP = 4
FOX_WIDE = 2
LOG2E = 1.4426950408889634
SMALL_W = 128


def _dot(a, b):
    return jnp.dot(a, b, preferred_element_type=F32)


def _dot_nt(a, b):
    return lax.dot_general(a, b, (((1,), (1,)), ((), ())), preferred_element_type=F32)


def _dot_tn(a, b):
    return lax.dot_general(a, b, (((0,), (0,)), ((), ())), preferred_element_type=F32)


def _rmsnorm(x, g):
    return x * lax.rsqrt(jnp.mean(x * x, axis=-1, keepdims=True) + EPS) * g


def _sigmoid(x):
    return 1.0 / (1.0 + jnp.exp(-x))


def _log_sigmoid(x):
    return jnp.minimum(x, 0.0) - jnp.log1p(jnp.exp(-jnp.abs(x)))


def _split3(x):
    hi = x.astype(BF16)
    r1 = x - hi.astype(F32)
    mid = r1.astype(BF16)
    lo = (r1 - mid.astype(F32)).astype(BF16)
    return hi, mid, lo


def _resident(shape):
    nd = len(shape)
    return pl.BlockSpec(shape, lambda *_: (0,) * nd, pipeline_mode=pl.Buffered(1))


def _params(semantics):
    return pltpu.CompilerParams(dimension_semantics=semantics, vmem_limit_bytes=V7X_VMEM_LIMIT)


def _row_tile(m, want):
    t = min(m, want)
    assert m % t == 0, (m, t)
    return t


def _ffn_body(x_ref, pre_ref, post_ref, wg_ref, wu_ref, wd_ref, o_ref, u_ref, acc_ref):
    x = x_ref[...]
    u_ref[...] = _rmsnorm(x, pre_ref[...]).astype(BF16)
    for c in range(wg_ref.shape[0]):
        u = u_ref[...]
        g = _dot(u, wg_ref[c])
        up = _dot(u, wu_ref[c])
        h = (g * _sigmoid(g) * up).astype(BF16)
        part = _dot(h, wd_ref[c])
        if c == 0:
            acc_ref[...] = part
        else:
            acc_ref[...] += part
    o_ref[...] = x + 0.5 * _rmsnorm(acc_ref[...], post_ref[...])


def _ffn(x2, pre_g, post_g, wg, wu, wd, tm=512):
    m, d = x2.shape
    tm = _row_tile(m, tm)
    return pl.pallas_call(
        _ffn_body,
        grid=(m // tm,),
        in_specs=[pl.BlockSpec((tm, d), lambda i: (i, 0)),
                  _resident(pre_g.shape), _resident(post_g.shape),
                  _resident(wg.shape), _resident(wu.shape), _resident(wd.shape)],
        out_specs=pl.BlockSpec((tm, d), lambda i: (i, 0)),
        out_shape=jax.ShapeDtypeStruct((m, d), F32),
        scratch_shapes=[pltpu.VMEM((tm, d), BF16), pltpu.VMEM((tm, d), F32)],
        compiler_params=_params(("parallel",)),
        name="ffn",
    )(x2, pre_g, post_g, wg, wu, wd)


def _mix_in_body(dims, n_carried, x_ref, pre_ref, w_ref, ws_ref, wa2_ref, ba_ref, bf_ref, *refs):
    gq_ref, gk_ref, gv_ref, la_ref, fq_ref, fk16_ref, fv16_ref, fk32_ref, fv32_ref, fl_ref, u_ref = refs[n_carried:]
    gk_w, gv_w, fx_w, dk_scale, hd_scale = dims
    u_ref[...] = _rmsnorm(x_ref[...], pre_ref[...]).astype(BF16)

    def proj(lo, width):
        return _dot(u_ref[...], w_ref[:, lo:lo + width])

    off = 0
    gq_ref[...] = proj(off, gk_w) * dk_scale
    off += gk_w
    gk_ref[...] = proj(off, gk_w)
    off += gk_w
    gv_ref[...] = proj(off, gv_w).astype(BF16)
    off += gv_w
    fq_ref[...] = (proj(off, fx_w) * hd_scale).astype(BF16)
    off += fx_w
    fk = proj(off, fx_w)
    fk32_ref[...] = fk
    fk16_ref[...] = fk.astype(BF16)
    off += fx_w
    fv = proj(off, fx_w)
    fv32_ref[...] = fv
    fv16_ref[...] = fv.astype(BF16)

    small = _dot(u_ref[...], ws_ref[...])
    lane = lax.broadcasted_iota(jnp.int32, small.shape, 1)
    fl_ref[...] = jnp.where(lane < FOX_HEADS, _log_sigmoid(small + bf_ref[...]), 0.0)
    lr = jnp.where((lane >= FOX_HEADS) & (lane < FOX_HEADS + GLA_RANK), small, 0.0).astype(BF16)
    la_ref[...] = _log_sigmoid(_dot(lr, wa2_ref[...]) + ba_ref[...]) * (1.0 / GLA_TAU)


def _mix_in(x2, pre_g, w1, ws, wa2p, b_a, b_f, layer, depth, carried, tm=512):
    m, d = x2.shape
    tm = _row_tile(m, tm)
    gk_w = d // 2
    dims = (gk_w, d, d, float((gk_w // GLA_HEADS) ** -0.5), float((d // FOX_HEADS) ** -0.5) * LOG2E)

    def rows(width):
        return pl.BlockSpec((tm, width), lambda i: (i, 0))

    def stacked(width):
        return pl.BlockSpec((None, tm, width), lambda i: (layer, i, 0))

    outs = [(gk_w, F32), (gk_w, F32), (d, BF16), (gk_w, F32), (d, BF16), (d, BF16), (d, BF16)]
    stacks = [d, d, SMALL_W]
    n_in = 7
    return pl.pallas_call(
        functools.partial(_mix_in_body, dims, len(carried)),
        grid=(m // tm,),
        in_specs=[rows(d), _resident(pre_g.shape), _resident(w1.shape), _resident(ws.shape),
                  _resident(wa2p.shape), _resident(b_a.shape), _resident(b_f.shape)]
                 + [pl.BlockSpec(memory_space=pl.ANY)] * len(carried),
        out_specs=[rows(w) for w, _ in outs] + [stacked(w) for w in stacks],
        out_shape=[jax.ShapeDtypeStruct((m, w), dt) for w, dt in outs]
                  + [jax.ShapeDtypeStruct((depth, m, w), F32) for w in stacks],
        input_output_aliases={n_in + j: len(outs) + j for j in range(len(carried))},
        scratch_shapes=[pltpu.VMEM((tm, d), BF16)],
        compiler_params=_params(("parallel",)),
        name="mix_in",
    )(x2, pre_g, w1, ws, wa2p, b_a, b_f, *carried)


def _cumsum_body(x_ref, m_ref, o_ref):
    x = x_ref[...]
    r = lax.broadcasted_iota(jnp.int32, (LANES, LANES), 0)
    c = lax.broadcasted_iota(jnp.int32, (LANES, LANES), 1)
    upper = jnp.where(r <= c, 1.0, 0.0).astype(BF16)
    hi, mid, lo = _split3(x)
    o_ref[...] = _dot(hi, upper) + _dot(mid, upper) + _dot(lo, upper)
    tot = jnp.broadcast_to(o_ref[:, LANES - 1:LANES], x.shape)
    hi, mid, lo = _split3(tot)
    mm = m_ref[...]
    offset = _dot(mm, hi) + _dot(mm, mid) + _dot(mm, lo)
    o_ref[...] = o_ref[...] + offset


def _cumsum_time(x3):
    g, n, _ = x3.shape
    per = 8
    while (per * n) % 8:
        per += 1
    gpb = per if g % per == 0 else g
    rb = gpb * n
    idx = np.arange(rb)
    earlier = (idx[:, None] // n == idx[None, :] // n) & (idx[None, :] < idx[:, None])
    mm = jnp.asarray(earlier.astype(np.float32), BF16)
    out = pl.pallas_call(
        _cumsum_body,
        grid=(g // gpb,),
        in_specs=[pl.BlockSpec((rb, LANES), lambda i: (i, 0)), _resident((rb, rb))],
        out_specs=pl.BlockSpec((rb, LANES), lambda i: (i, 0)),
        out_shape=jax.ShapeDtypeStruct((g * n, LANES), F32),
        compiler_params=_params(("parallel",)),
        name="cumsum",
    )(x3.reshape(g * n, LANES), mm)
    return out.reshape(g, n, LANES)


def _gla_chunk(q, k, v, la, st):
    cl, dk = q.shape
    sub = min(GLA_SUB, cl)
    row = lax.broadcasted_iota(jnp.int32, (cl, cl), 0)
    col = lax.broadcasted_iota(jnp.int32, (cl, cl), 1)
    lower = jnp.where(row >= col, 1.0, 0.0).astype(BF16)
    hi, mid, lo = _split3(la)
    b = _dot(lower, hi) + _dot(lower, mid) + _dot(lower, lo)
    b_end = b[cl - 1:cl, :]

    o = _dot_nt((q * jnp.exp(b)).astype(BF16), st.astype(BF16))

    tok = lax.broadcasted_iota(jnp.int32, (cl, dk), 0)
    tsub = lax.broadcasted_iota(jnp.int32, (SUBLANES, dk), 0)
    acol = lax.broadcasted_iota(jnp.int32, (SUBLANES, cl), 1)
    blocks = []
    for i in range(cl // sub):
        r0 = i * sub
        b_i, q_i, k_i = b[r0:r0 + sub], q[r0:r0 + sub], k[r0:r0 + sub]
        groups = [jnp.zeros((SUBLANES, cl), F32) for _ in range(sub // SUBLANES)]
        for s in range(sub):
            own = s // SUBLANES
            for gi in range(own, sub // SUBLANES):
                rows = slice(gi * SUBLANES, (gi + 1) * SUBLANES)
                rel = b_i[rows] - b_i[s:s + 1]
                if gi == own:
                    rel = jnp.where(tsub >= s - gi * SUBLANES, rel, -jnp.inf)
                w = jnp.sum(q_i[rows] * k_i[s:s + 1] * jnp.exp(rel), axis=1, keepdims=True)
                groups[gi] = jnp.where(acol == r0 + s, w, groups[gi])
        a_i = jnp.concatenate(groups, axis=0)
        if i > 0:
            ref = b[r0 - 1:r0]
            q_t = (q_i * jnp.exp(b_i - ref)).astype(BF16)
            k_t = (k * jnp.exp(jnp.where(tok < r0, ref - b, -jnp.inf))).astype(BF16)
            a_i = a_i + _dot_nt(q_t, k_t)
        blocks.append(a_i)
    attn = jnp.concatenate(blocks, axis=0).astype(BF16)
    o = o + _dot(attn, v)

    k_d = (k * jnp.exp(b_end - b)).astype(BF16)
    st_new = st * jnp.exp(b_end) + _dot_tn(v, k_d)
    return o, st_new


def _gla_body(cl, q_ref, k_ref, v_ref, la_ref, s0_ref, o_ref, sf_ref, st_ref):
    t = pl.program_id(1)
    heads, dv, dk = st_ref.shape

    @pl.when(t == 0)
    def _():
        for hh in range(heads):
            st_ref[hh] = s0_ref[0, hh].T

    n_chunks = q_ref.shape[1] // cl

    def chunk(ci, carry):
        r = pl.multiple_of(ci * cl, cl)
        for hh in range(heads):
            ks = slice(hh * dk, (hh + 1) * dk)
            vs = slice(hh * dv, (hh + 1) * dv)
            o, st = _gla_chunk(q_ref[0, pl.ds(r, cl), ks], k_ref[0, pl.ds(r, cl), ks], v_ref[0, pl.ds(r, cl), vs],
                               la_ref[0, pl.ds(r, cl), ks], st_ref[hh])
            o_ref[0, pl.ds(r, cl), vs] = o.astype(o_ref.dtype)
            st_ref[hh] = st
        return carry

    lax.fori_loop(0, n_chunks, chunk, 0, unroll=2 if n_chunks % 2 == 0 else 1)

    @pl.when(t == pl.num_programs(1) - 1)
    def _():
        for hh in range(heads):
            sf_ref[0, hh] = st_ref[hh].T


def _gla(gq, gk, gv, la, s0, s0_off=0, tt=512):
    b, t, _ = gq.shape
    _, h, dk, dv = s0.shape
    cl = min(GLA_CHUNK, t)
    tt = _row_tile(t, tt)
    return pl.pallas_call(
        functools.partial(_gla_body, cl),
        grid=(b, t // tt),
        in_specs=[pl.BlockSpec((1, tt, h * dk), lambda i, n: (i, n, 0)),
                  pl.BlockSpec((1, tt, h * dk), lambda i, n: (i, n, 0)),
                  pl.BlockSpec((1, tt, h * dv), lambda i, n: (i, n, 0)),
                  pl.BlockSpec((1, tt, h * dk), lambda i, n: (i, n, 0)),
                  pl.BlockSpec((1, h, dk, dv), lambda i, n: (i + s0_off, 0, 0, 0))],
        out_specs=[pl.BlockSpec((1, tt, h * dv), lambda i, n: (i, n, 0)),
                   pl.BlockSpec((1, h, dk, dv), lambda i, n: (i, 0, 0, 0))],
        out_shape=[jax.ShapeDtypeStruct((b, t, h * dv), BF16), jax.ShapeDtypeStruct((b, h, dk, dv), F32)],
        scratch_shapes=[pltpu.VMEM((h, dv, dk), F32)],
        compiler_params=_params(("parallel", "arbitrary")),
        name="gla",
    )(gq, gk, gv, la, s0)


def _head_column(block, head):
    lane = lax.broadcasted_iota(jnp.int32, block.shape, 1)
    return jnp.sum(jnp.where(lane == head, block, 0.0), axis=1, keepdims=True)


def _fox_prompt_body(hps, q_ref, k_ref, v_ref, o_ref, m_ref, acc_ref):
    qi = pl.program_id(2)
    tq = q_ref.shape[1]
    hw = q_ref.shape[2] // hps
    hd = v_ref.shape[2] // hps
    wide = FOX_WIDE * tq

    def logits(hh, r, width):
        sl = slice(hh * hw, (hh + 1) * hw)
        return _dot_nt(q_ref[0, :, sl], k_ref[0, pl.ds(r, width), sl])

    def values(hh, r, width):
        lane = lax.broadcasted_iota(jnp.int32, (width, LANES), 1)
        ones_col = jnp.where(lane == 0, 1.0, 0.0).astype(BF16)
        return jnp.concatenate([v_ref[0, pl.ds(r, width), hh * hd:(hh + 1) * hd], ones_col], axis=1)

    def first(r, width):
        row = lax.broadcasted_iota(jnp.int32, (tq, width), 0)
        col = lax.broadcasted_iota(jnp.int32, (tq, width), 1)
        for hh in range(hps):
            z = jnp.where(col <= row + (width - tq), logits(hh, r, width), -jnp.inf)
            m = jnp.max(z, axis=1, keepdims=True)
            m_ref[hh] = m
            acc_ref[hh] = _dot(jnp.exp2(z - m).astype(BF16), values(hh, r, width))

    assert FOX_WIDE == 2
    lead = (qi + 1) % FOX_WIDE

    @pl.when(lead == 1)
    def _():
        first(pl.multiple_of(qi * tq, tq), tq)

    @pl.when(lead == 0)
    def _():
        first(pl.multiple_of((qi + 1) * tq - wide, tq), wide)

    def update(r, width):
        for hh in range(hps):
            z = logits(hh, r, width)
            m_old = m_ref[hh]
            m_new = jnp.maximum(m_old, jnp.max(z, axis=1, keepdims=True))
            m_ref[hh] = m_new
            pv = _dot(jnp.exp2(z - m_new).astype(BF16), values(hh, r, width))
            acc_ref[hh] = jnp.exp2(m_old - m_new) * acc_ref[hh] + pv

    def wide_block(j, carry):
        update(pl.multiple_of(j * wide, wide), wide)
        return carry

    lax.fori_loop(0, qi // FOX_WIDE, wide_block, 0)
    for hh in range(hps):
        acc = acc_ref[hh]
        o_ref[0, :, hh * hd:(hh + 1) * hd] = (acc[:, :hd] / acc[:, hd:hd + 1]).astype(o_ref.dtype)


def _fox_augment(x, f_cols, query):
    b, t, d = x.shape
    h = f_cols.shape[2]
    hd = d // h

    def top16(v):
        return lax.bitcast_convert_type(lax.bitcast_convert_type(v, jnp.uint32) & jnp.uint32(0xFFFF0000), F32)

    f = f_cols * LOG2E
    hi = top16(f)
    mid = top16(f - hi)
    lo = f - hi - mid
    pieces = jnp.stack([hi, mid, lo], axis=-1)
    ones = jnp.ones_like(pieces)
    extra = jnp.concatenate([pieces, ones] if query else [ones, -pieces], axis=-1).astype(BF16)
    extra = jnp.pad(extra, ((0, 0), (0, 0), (0, 0), (0, hd - extra.shape[-1])))
    return jnp.concatenate([x.reshape(b, t, h, hd), extra], axis=-1).reshape(b, t, 2 * d)


def _fox_prompt(q_aug, k_aug, fv, tq=512, hps=FOX_HEADS_PER_STEP):
    b, t, d = fv.shape
    h = FOX_HEADS
    hd = d // h
    nq = t // tq
    keys = pl.BlockSpec((1, t, hps * 2 * hd), lambda i, j, n: (i, 0, j), pipeline_mode=pl.Buffered(1))
    vals = pl.BlockSpec((1, t, hps * hd), lambda i, j, n: (i, 0, j), pipeline_mode=pl.Buffered(1))
    return pl.pallas_call(
        functools.partial(_fox_prompt_body, hps),
        grid=(b, h // hps, nq),
        in_specs=[pl.BlockSpec((1, tq, hps * 2 * hd), lambda i, j, n: (i, n, j)), keys, vals],
        out_specs=pl.BlockSpec((1, tq, hps * hd), lambda i, j, n: (i, n, j)),
        out_shape=jax.ShapeDtypeStruct((b, t, d), BF16),
        scratch_shapes=[pltpu.VMEM((hps, tq, 1), F32), pltpu.VMEM((hps, tq, hd + LANES), F32)],
        compiler_params=_params(("parallel", "parallel", "arbitrary")),
        name="fox_prompt",
    )(q_aug, k_aug, fv)


def _fox_sample_body(q_ref, kc_ref, vc_ref, kn_ref, vn_ref, fq_ref, fk_ref, o_ref):
    head = pl.program_id(1)
    q = q_ref[0]
    ln = q.shape[0]
    past = kc_ref.shape[1]
    f_q = _head_column(fq_ref[0], head) * LOG2E
    f_k = fk_ref[0, 0] * LOG2E
    z_c = _dot_nt(q, kc_ref[0].astype(BF16)) + f_q - f_k[:, :past]
    z_n = _dot_nt(q, kn_ref[0]) + f_q - f_k[:, past:past + ln]
    row = lax.broadcasted_iota(jnp.int32, z_n.shape, 0)
    col = lax.broadcasted_iota(jnp.int32, z_n.shape, 1)
    z_n = jnp.where(col <= row, z_n, -jnp.inf)
    m = jnp.maximum(jnp.max(z_c, axis=1, keepdims=True), jnp.max(z_n, axis=1, keepdims=True))
    p_c = jnp.exp2(z_c - m)
    p_n = jnp.exp2(z_n - m)
    denom = jnp.sum(p_c, axis=1, keepdims=True) + jnp.sum(p_n, axis=1, keepdims=True)
    o = _dot(p_c.astype(BF16), vc_ref[0].astype(BF16)) + _dot(p_n.astype(BF16), vn_ref[0])
    o_ref[0] = (o / denom).astype(o_ref.dtype)


def _fox_sample(fq, fk, fv, cache_k, cache_v, cache_off, f_cols, f_rows):
    b, ln, d = fq.shape
    past = cache_k.shape[1]
    h = f_cols.shape[2]
    hd = d // h
    fw = f_rows.shape[3]
    return pl.pallas_call(
        _fox_sample_body,
        grid=(b, h),
        in_specs=[pl.BlockSpec((1, ln, hd), lambda i, j: (i, 0, j)),
                  pl.BlockSpec((1, past, hd), lambda i, j: (i + cache_off, 0, j)),
                  pl.BlockSpec((1, past, hd), lambda i, j: (i + cache_off, 0, j)),
                  pl.BlockSpec((1, ln, hd), lambda i, j: (i, 0, j)),
                  pl.BlockSpec((1, ln, hd), lambda i, j: (i, 0, j)),
                  pl.BlockSpec((1, ln, h), lambda i, j: (i, 0, 0)),
                  pl.BlockSpec((1, 1, 1, fw), lambda i, j: (i, j, 0, 0))],
        out_specs=pl.BlockSpec((1, ln, hd), lambda i, j: (i, 0, j)),
        out_shape=jax.ShapeDtypeStruct((b, ln, d), BF16),
        compiler_params=_params(("parallel", "parallel")),
        name="fox_sample",
    )(fq, cache_k, cache_v, fk, fv, f_cols, f_rows)


def _mem_kv_body(mem_ref, g_ref, w_ref, k_ref, v_ref):
    u = _rmsnorm(mem_ref[0], g_ref[...]).astype(BF16)
    d = k_ref.shape[2]
    k_ref[0] = _dot(u, w_ref[:, :d])
    v_ref[0] = _dot(u, w_ref[:, d:])


def _mem_kv(mem, g, w):
    b, n, d = mem.shape
    blk = pl.BlockSpec((1, n, d), lambda i: (i, 0, 0))
    return pl.pallas_call(
        _mem_kv_body,
        grid=(b,),
        in_specs=[blk, _resident(g.shape), _resident(w.shape)],
        out_specs=[blk, blk],
        out_shape=[jax.ShapeDtypeStruct((b, n, d), F32)] * 2,
        compiler_params=_params(("parallel",)),
        name="mem_kv",
    )(mem, g, w)


def _mix_out_body(seq, x_ref, og_ref, of_ref, mk_ref, mv_ref, pre_ref, w2_ref, gn_ref, wgo_ref, wfo_ref, wmo_ref,
                  wout_ref, post_ref, o_ref, u_ref):
    x = x_ref[...]
    tm, d = x.shape
    u_ref[...] = _rmsnorm(x, pre_ref[...]).astype(BF16)

    def proj(j):
        return _dot(u_ref[...], w2_ref[:, j * d:(j + 1) * d])

    mq = proj(1)
    hd = d // MEM_HEADS
    scale = float(hd ** -0.5)
    per_batch = []
    for j in range(tm // seq):
        mk = mk_ref[j].astype(BF16)
        mv = mv_ref[j].astype(BF16)
        heads = []
        for hh in range(MEM_HEADS):
            sl = slice(hh * hd, (hh + 1) * hd)
            s = _dot_nt(mq[j * seq:(j + 1) * seq, sl].astype(BF16), mk[:, sl]) * scale
            e = jnp.exp(s - jnp.max(s, axis=1, keepdims=True))
            p = e / jnp.sum(e, axis=1, keepdims=True)
            heads.append(_dot(p.astype(BF16), mv[:, sl]))
        per_batch.append(jnp.concatenate(heads, axis=1))
    o_mem = per_batch[0] if len(per_batch) == 1 else jnp.concatenate(per_batch, axis=0)
    y = _sigmoid(proj(4)) * _dot(o_mem.astype(BF16), wmo_ref[...])

    y = y + _sigmoid(proj(3)) * _dot(of_ref[...], wfo_ref[...])

    og = og_ref[...].astype(F32)
    gd = d // GLA_HEADS
    gn = gn_ref[...]
    normed = []
    for hh in range(GLA_HEADS):
        sl = slice(hh * gd, (hh + 1) * gd)
        normed.append(_rmsnorm(og[:, sl], gn[:, sl]))
    r = proj(0)
    b_gla = _dot((jnp.concatenate(normed, axis=1) * (r * _sigmoid(r))).astype(BF16), wgo_ref[...])
    y = y + _sigmoid(proj(2)) * b_gla

    mixed = _dot(y.astype(BF16), wout_ref[...])
    o_ref[...] = x + _rmsnorm(mixed, post_ref[...])


def _mix_out(x2, o_gla, o_fox, mk, mv, mem_off, seq, pre_g, w2, gn, wgo, wfo, wmo, wout, post_g, tm=256):
    m, d = x2.shape
    if seq < tm:
        nb = 4
        while (m // seq) % nb:
            nb //= 2
        tm = seq * nb
    assert m % tm == 0 and (seq % tm == 0 or tm % seq == 0)
    nb = max(1, tm // seq)
    steps_per_batch = max(1, seq // tm)
    mem = mk.shape[1]
    rows = pl.BlockSpec((tm, d), lambda i: (i, 0))
    assert mem_off % nb == 0
    memb = pl.BlockSpec((nb, mem, d), lambda i: (i // steps_per_batch + mem_off // nb, 0, 0))
    return pl.pallas_call(
        functools.partial(_mix_out_body, min(seq, tm)),
        grid=(m // tm,),
        in_specs=[rows, rows, rows, memb, memb, _resident(pre_g.shape), _resident(w2.shape), _resident(gn.shape),
                  _resident(wgo.shape), _resident(wfo.shape), _resident(wmo.shape), _resident(wout.shape),
                  _resident(post_g.shape)],
        out_specs=rows,
        out_shape=jax.ShapeDtypeStruct((m, d), F32),
        scratch_shapes=[pltpu.VMEM((tm, d), BF16)],
        compiler_params=_params(("parallel",)),
        name="mix_out",
    )(x2, o_gla, o_fox, mk, mv, pre_g, w2, gn, wgo, wfo, wmo, wout, post_g)


def _prep_layer(d, w):
    gk_w = d // 2
    sizes = (gk_w, gk_w, d, GLA_RANK, d, d, d, d, FOX_HEADS, d, N_BRANCH * d)
    offs = np.concatenate([[0], np.cumsum(sizes)]).tolist()
    seg = {n: (offs[i], offs[i + 1]) for i, n in enumerate(
        ("gq", "gk", "gv", "glr", "gr", "fq", "fk", "fv", "ff", "mq", "gates"))}
    w_in = w["w_in"]

    def cols(*names):
        return jnp.concatenate([w_in[:, seg[n][0]:seg[n][1]] for n in names], axis=1).astype(BF16)

    ws = jnp.pad(cols("ff", "glr"), ((0, 0), (0, SMALL_W - FOX_HEADS - GLA_RANK)))
    wa2p = jnp.pad(w["gla_w_a2"].astype(BF16), ((FOX_HEADS, SMALL_W - FOX_HEADS - GLA_RANK), (0, 0)))
    b_f = jnp.pad(w["fox_b_f"], (0, SMALL_W - FOX_HEADS)).reshape(1, SMALL_W)

    def ffn(pfx):
        wg, wu, wd = w[pfx + "_w_gate"], w[pfx + "_w_up"], w[pfx + "_w_down"]
        nc = wg.shape[1] // FFN_CHUNK
        return dict(pre=w[pfx + "_pre_g"].reshape(1, d), post=w[pfx + "_post_g"].reshape(1, d),
                    wg=wg.astype(BF16).reshape(d, nc, FFN_CHUNK).transpose(1, 0, 2),
                    wu=wu.astype(BF16).reshape(d, nc, FFN_CHUNK).transpose(1, 0, 2),
                    wd=wd.astype(BF16).reshape(nc, FFN_CHUNK, d))

    return dict(
        ffn1=ffn("ffn1"), ffn2=ffn("ffn2"),
        mix_pre=w["mix_pre_g"].reshape(1, d), mix_post=w["mix_post_g"].reshape(1, d),
        w1=cols("gq", "gk", "gv", "fq", "fk", "fv"), ws=ws, wa2p=wa2p,
        b_a=w["gla_b_a"].reshape(1, gk_w), b_f=b_f,
        w2=cols("gr", "mq", "gates"),
        gn=w["gla_norm_g"].reshape(1, d),
        wgo=w["w_gla_o"].astype(BF16), wfo=w["w_fox_o"].astype(BF16), wmo=w["w_mem_o"].astype(BF16),
        wout=w["w_out"].astype(BF16),
        mem_g=w["mem_norm_g"].reshape(1, d), w_mem_kv=w["w_mem_kv"].astype(BF16),
    )


def _ffn_apply(x2, p):
    return _ffn(x2, p["pre"], p["post"], p["wg"], p["wu"], p["wd"])


def _forget_layouts(f, lead=None):
    b, t, h = f.shape
    if lead is not None:
        f = jnp.concatenate([lead.astype(F32), f], axis=1)
    total = f.shape[1]
    padded = -(-total // LANES) * LANES
    f = jnp.pad(f, ((0, 0), (0, padded - total), (0, 0)))
    rows = _cumsum_time(jnp.swapaxes(f, 1, 2).reshape(b * h, padded // LANES, LANES))
    rows = rows.reshape(b, h, padded)
    cols = jnp.swapaxes(rows[:, :, total - t:total], 1, 2)
    return cols, rows


def _group_layer(x2, b, t, p, layer, depth, carried, gla_state, fox, memory):
    d = x2.shape[1]
    x2 = _ffn_apply(x2, p["ffn1"])
    gq, gk, gv, la, fq, fk16, fv16, *carried = _mix_in(x2, p["mix_pre"], p["w1"], p["ws"], p["wa2p"], p["b_a"],
                                                       p["b_f"], layer, depth, carried)
    r3 = lambda a: a.reshape(b, t, a.shape[1])
    o_gla, s_fin = _gla(r3(gq), r3(gk), r3(gv), r3(la), gla_state[0], gla_state[1])
    f_log = carried[2][layer].reshape(b, t, SMALL_W)[:, :, :FOX_HEADS]
    o_fox = fox(r3(fq), r3(fk16), r3(fv16), f_log)
    x2 = _mix_out(x2, o_gla.reshape(b * t, d), o_fox.reshape(b * t, d), memory[0], memory[1], memory[2], t,
                  p["mix_pre"], p["w2"], p["gn"], p["wgo"], p["wfo"], p["wmo"], p["wout"], p["mix_post"])
    x2 = _ffn_apply(x2, p["ffn2"])
    return x2, carried, s_fin


def kernel(x_prompt, x_sample, cache_fox_k, cache_fox_v, cache_fox_logf, state_gla, cache_mem_k, cache_mem_v, mem_prompt, ffn1_pre_g, ffn1_post_g, ffn1_w_gate, ffn1_w_up, ffn1_w_down, mix_pre_g, mix_post_g, w_in, gla_w_a2, gla_b_a, fox_b_f, gla_norm_g, w_gla_o, w_fox_o, w_mem_o, w_out, mem_norm_g, w_mem_kv, ffn2_pre_g, ffn2_post_g, ffn2_w_gate, ffn2_w_up, ffn2_w_down):
    weights = dict(ffn1_pre_g=ffn1_pre_g, ffn1_post_g=ffn1_post_g, ffn1_w_gate=ffn1_w_gate, ffn1_w_up=ffn1_w_up,
                   ffn1_w_down=ffn1_w_down, mix_pre_g=mix_pre_g, mix_post_g=mix_post_g, w_in=w_in,
                   gla_w_a2=gla_w_a2, gla_b_a=gla_b_a, fox_b_f=fox_b_f, gla_norm_g=gla_norm_g, w_gla_o=w_gla_o,
                   w_fox_o=w_fox_o, w_mem_o=w_mem_o, w_out=w_out, mem_norm_g=mem_norm_g, w_mem_kv=w_mem_kv,
                   ffn2_pre_g=ffn2_pre_g, ffn2_post_g=ffn2_post_g, ffn2_w_gate=ffn2_w_gate, ffn2_w_up=ffn2_w_up,
                   ffn2_w_down=ffn2_w_down)
    depth = w_in.shape[0]
    bp, tp, d = x_prompt.shape
    bs, ts, _ = x_sample.shape
    hd = d // FOX_HEADS
    mh = d // MEM_HEADS
    dk, dv = state_gla.shape[-2:]
    layers = [_prep_layer(d, {k: v[l] for k, v in weights.items()}) for l in range(depth)]

    x = x_prompt.reshape(bp * tp, d)
    zero_state = jnp.zeros((bp, GLA_HEADS, dk, dv), F32)
    tq = min(512, tp)

    def fox_p(fq, fk, fv, f_log):
        f_cols, _ = _forget_layouts(f_log)
        return _fox_prompt(_fox_augment(fq, f_cols, True), _fox_augment(fk, f_cols, False), fv, tq=tq)

    carried, p_state, p_mk, p_mv = [], [], [], []
    for l in range(depth):
        mk, mv = _mem_kv(mem_prompt, layers[l]["mem_g"], layers[l]["w_mem_kv"])
        x, carried, s_fin = _group_layer(x, bp, tp, layers[l], l, depth, carried, (zero_state, 0), fox_p, (mk, mv, 0))
        p_state.append(s_fin)
        p_mk.append(mk)
        p_mv.append(mv)
    y_prompt = x.reshape(bp, tp, d)
    p_fox_k = carried[0].reshape(depth, bp, tp, FOX_HEADS, hd)
    p_fox_v = carried[1].reshape(depth, bp, tp, FOX_HEADS, hd)
    p_fox_logf = carried[2].reshape(depth, bp, tp, SMALL_W)[..., :FOX_HEADS]
    mem_len = mem_prompt.shape[1]
    p_mem_k = jnp.stack(p_mk).reshape(depth, bp, mem_len, MEM_HEADS, mh)
    p_mem_v = jnp.stack(p_mv).reshape(depth, bp, mem_len, MEM_HEADS, mh)

    x = x_sample.reshape(bs * ts, d)
    past = cache_fox_k.shape[2]
    ck = cache_fox_k.reshape(depth * bs, past, d)
    cv = cache_fox_v.reshape(depth * bs, past, d)
    states = state_gla.reshape(depth * bs, GLA_HEADS, dk, dv)
    mem_s = cache_mem_k.shape[2]
    cmk = cache_mem_k.reshape(depth * bs, mem_s, d)
    cmv = cache_mem_v.reshape(depth * bs, mem_s, d)
    carried, s_state = [], []
    for l in range(depth):
        def fox_s(fq, fk, fv, f_log, l=l):
            f_cols, f_rows = _forget_layouts(f_log, lead=cache_fox_logf[l])
            return _fox_sample(fq, fk, fv, ck, cv, l * bs, f_cols, f_rows[:, :, None, :])

        x, carried, s_new = _group_layer(x, bs, ts, layers[l], l, depth, carried, (states, l * bs), fox_s,
                                         (cmk, cmv, l * bs))
        s_state.append(s_new)
    y_sample = x.reshape(bs, ts, d)
    s_fox_k = carried[0].reshape(depth, bs, ts, FOX_HEADS, hd)
    s_fox_v = carried[1].reshape(depth, bs, ts, FOX_HEADS, hd)
    s_fox_logf = carried[2].reshape(depth, bs, ts, SMALL_W)[..., :FOX_HEADS]

    return (y_prompt, y_sample, p_fox_k, p_fox_v, p_fox_logf, jnp.stack(p_state), p_mem_k, p_mem_v,
            s_fox_k, s_fox_v, s_fox_logf, jnp.stack(s_state))
```

```python
import functools

import numpy as np
import jax
import jax.numpy as jnp
from jax import lax
from jax.experimental import pallas as pl
from jax.experimental.pallas import tpu as pltpu

F32 = jnp.float32
BF16 = jnp.bfloat16

EPS = 1e-6
GLA_HEADS = 4
GLA_RANK = 16
GLA_TAU = 16.0
FOX_HEADS = 8
MEM_HEADS = 4
N_BRANCH = 3

LANES = 128
SUBLANES = 8
V7X_VMEM_LIMIT = 56 * 1024 * 1024
FFN_CHUNK = 256
GLA_CHUNK = 64
GLA_SUB = 16
FOX_HEADS_PER_STEP = 4
FOX_WIDE = 2
LOG2E = 1.4426950408889634
SMALL_W = 128


def _dot(a, b):
    return jnp.dot(a, b, preferred_element_type=F32)


def _dot_nt(a, b):
    return lax.dot_general(a, b, (((1,), (1,)), ((), ())), preferred_element_type=F32)


def _dot_tn(a, b):
    return lax.dot_general(a, b, (((0,), (0,)), ((), ())), preferred_element_type=F32)


def _rmsnorm(x, g):
    return x * lax.rsqrt(jnp.mean(x * x, axis=-1, keepdims=True) + EPS) * g


def _sigmoid(x):
    return 1.0 / (1.0 + jnp.exp(-x))


def _log_sigmoid(x):
    return jnp.minimum(x, 0.0) - jnp.log1p(jnp.exp(-jnp.abs(x)))


def _split3(x):
    hi = x.astype(BF16)
    r1 = x - hi.astype(F32)
    mid = r1.astype(BF16)
    lo = (r1 - mid.astype(F32)).astype(BF16)
    return hi, mid, lo


def _resident(shape):
    nd = len(shape)
    return pl.BlockSpec(shape, lambda *_: (0,) * nd, pipeline_mode=pl.Buffered(1))


def _params(semantics):
    return pltpu.CompilerParams(dimension_semantics=semantics, vmem_limit_bytes=V7X_VMEM_LIMIT)


def _row_tile(m, want):
    t = min(m, want)
    assert m % t == 0, (m, t)
    return t


def _ffn_body(x_ref, pre_ref, post_ref, wg_ref, wu_ref, wd_ref, o_ref, u_ref, acc_ref):
    x = x_ref[...]
    u_ref[...] = _rmsnorm(x, pre_ref[...]).astype(BF16)
    for c in range(wg_ref.shape[1] // FFN_CHUNK):
        cols = slice(c * FFN_CHUNK, (c + 1) * FFN_CHUNK)
        u = u_ref[...]
        g = _dot(u, wg_ref[:, cols])
        up = _dot(u, wu_ref[:, cols])
        h = (g * _sigmoid(g) * up).astype(BF16)
        part = _dot(h, wd_ref[cols, :])
        if c == 0:
            acc_ref[...] = part
        else:
            acc_ref[...] += part
    o_ref[...] = x + 0.5 * _rmsnorm(acc_ref[...], post_ref[...])


def _ffn(x2, pre_g, post_g, wg, wu, wd, tm=512):
    m, d = x2.shape
    tm = _row_tile(m, tm)
    return pl.pallas_call(
        _ffn_body,
        grid=(m // tm,),
        in_specs=[pl.BlockSpec((tm, d), lambda i: (i, 0)),
                  _resident(pre_g.shape), _resident(post_g.shape),
                  _resident(wg.shape), _resident(wu.shape), _resident(wd.shape)],
        out_specs=pl.BlockSpec((tm, d), lambda i: (i, 0)),
        out_shape=jax.ShapeDtypeStruct((m, d), F32),
        scratch_shapes=[pltpu.VMEM((tm, d), BF16), pltpu.VMEM((tm, d), F32)],
        compiler_params=_params(("parallel",)),
        name="ffn",
    )(x2, pre_g, post_g, wg, wu, wd)


def _mix_in_body(dims, n_carried, x_ref, pre_ref, w_ref, ws_ref, wa2_ref, ba_ref, bf_ref, *refs):
    gq_ref, gk_ref, gv_ref, la_ref, fq_ref, fk16_ref, fv16_ref, fk32_ref, fv32_ref, fl_ref, u_ref = refs[n_carried:]
    gk_w, gv_w, fx_w, dk_scale, hd_scale = dims
    u_ref[...] = _rmsnorm(x_ref[...], pre_ref[...]).astype(BF16)

    def proj(lo, width):
        return _dot(u_ref[...], w_ref[:, lo:lo + width])

    off = 0
    gq_ref[...] = proj(off, gk_w) * dk_scale
    off += gk_w
    gk_ref[...] = proj(off, gk_w)
    off += gk_w
    gv_ref[...] = proj(off, gv_w).astype(BF16)
    off += gv_w
    fq_ref[...] = (proj(off, fx_w) * hd_scale).astype(BF16)
    off += fx_w
    hd = fx_w // FOX_HEADS
    fk = proj(off, fx_w)
    fk16_ref[...] = fk.astype(BF16)
    off += fx_w
    fv = proj(off, fx_w)
    fv16_ref[...] = fv.astype(BF16)
    for hh in range(FOX_HEADS):
        fk32_ref[:, hh, :] = fk[:, hh * hd:(hh + 1) * hd]
        fv32_ref[:, hh, :] = fv[:, hh * hd:(hh + 1) * hd]

    small = _dot(u_ref[...], ws_ref[...])
    lane = lax.broadcasted_iota(jnp.int32, small.shape, 1)
    fl_ref[...] = jnp.where(lane < FOX_HEADS, _log_sigmoid(small + bf_ref[...]), 0.0)
    lr = jnp.where((lane >= FOX_HEADS) & (lane < FOX_HEADS + GLA_RANK), small, 0.0).astype(BF16)
    la_ref[...] = _log_sigmoid(_dot(lr, wa2_ref[...]) + ba_ref[...]) * (1.0 / GLA_TAU)


def _mix_in(x2, pre_g, w1, ws, wa2p, b_a, b_f, layer, depth, carried, tm=512):
    m, d = x2.shape
    tm = _row_tile(m, tm)
    gk_w = d // 2
    dims = (gk_w, d, d, float((gk_w // GLA_HEADS) ** -0.5), float((d // FOX_HEADS) ** -0.5) * LOG2E)

    def rows(width):
        return pl.BlockSpec((tm, width), lambda i: (i, 0))

    def stacked(*tail):
        return pl.BlockSpec((None, tm) + tail, lambda i: (layer, i) + (0,) * len(tail))

    outs = [(gk_w, F32), (gk_w, F32), (d, BF16), (gk_w, F32), (d, BF16), (d, BF16), (d, BF16)]
    stacks = [(FOX_HEADS, d // FOX_HEADS), (FOX_HEADS, d // FOX_HEADS), (SMALL_W,)]
    n_in = 7
    return pl.pallas_call(
        functools.partial(_mix_in_body, dims, len(carried)),
        grid=(m // tm,),
        in_specs=[rows(d), _resident(pre_g.shape), _resident(w1.shape), _resident(ws.shape),
                  _resident(wa2p.shape), _resident(b_a.shape), _resident(b_f.shape)]
                 + [pl.BlockSpec(memory_space=pl.ANY)] * len(carried),
        out_specs=[rows(w) for w, _ in outs] + [stacked(*tail) for tail in stacks],
        out_shape=[jax.ShapeDtypeStruct((m, w), dt) for w, dt in outs]
                  + [jax.ShapeDtypeStruct((depth, m) + tail, F32) for tail in stacks],
        input_output_aliases={n_in + j: len(outs) + j for j in range(len(carried))},
        scratch_shapes=[pltpu.VMEM((tm, d), BF16)],
        compiler_params=_params(("parallel",)),
        name="mix_in",
    )(x2, pre_g, w1, ws, wa2p, b_a, b_f, *carried)


def _cumsum_body(x_ref, m_ref, o_ref):
    x = x_ref[...]
    r = lax.broadcasted_iota(jnp.int32, (LANES, LANES), 0)
    c = lax.broadcasted_iota(jnp.int32, (LANES, LANES), 1)
    upper = jnp.where(r <= c, 1.0, 0.0).astype(BF16)
    hi, mid, lo = _split3(x)
    o_ref[...] = _dot(hi, upper) + _dot(mid, upper) + _dot(lo, upper)
    tot = jnp.broadcast_to(o_ref[:, LANES - 1:LANES], x.shape)
    hi, mid, lo = _split3(tot)
    mm = m_ref[...]
    offset = _dot(mm, hi) + _dot(mm, mid) + _dot(mm, lo)
    o_ref[...] = o_ref[...] + offset


def _cumsum_time(x3):
    g, n, _ = x3.shape
    per = 8
    while (per * n) % 8:
        per += 1
    gpb = per if g % per == 0 else g
    rb = gpb * n
    idx = np.arange(rb)
    earlier = (idx[:, None] // n == idx[None, :] // n) & (idx[None, :] < idx[:, None])
    mm = jnp.asarray(earlier.astype(np.float32), BF16)
    out = pl.pallas_call(
        _cumsum_body,
        grid=(g // gpb,),
        in_specs=[pl.BlockSpec((rb, LANES), lambda i: (i, 0)), _resident((rb, rb))],
        out_specs=pl.BlockSpec((rb, LANES), lambda i: (i, 0)),
        out_shape=jax.ShapeDtypeStruct((g * n, LANES), F32),
        compiler_params=_params(("parallel",)),
        name="cumsum",
    )(x3.reshape(g * n, LANES), mm)
    return out.reshape(g, n, LANES)


def _gla_chunk(q, k, v, la, st):
    cl, dk = q.shape
    sub = min(GLA_SUB, cl)
    row = lax.broadcasted_iota(jnp.int32, (cl, cl), 0)
    col = lax.broadcasted_iota(jnp.int32, (cl, cl), 1)
    lower = jnp.where(row >= col, 1.0, 0.0).astype(BF16)
    hi, mid, lo = _split3(la)
    b = _dot(lower, hi) + _dot(lower, mid) + _dot(lower, lo)
    b_end = b[cl - 1:cl, :]

    o = _dot_nt((q * jnp.exp(b)).astype(BF16), st.astype(BF16))

    tok = lax.broadcasted_iota(jnp.int32, (cl, dk), 0)
    tsub = lax.broadcasted_iota(jnp.int32, (SUBLANES, dk), 0)
    acol = lax.broadcasted_iota(jnp.int32, (SUBLANES, cl), 1)
    blocks = []
    for i in range(cl // sub):
        r0 = i * sub
        b_i, q_i, k_i = b[r0:r0 + sub], q[r0:r0 + sub], k[r0:r0 + sub]
        groups = [jnp.zeros((SUBLANES, cl), F32) for _ in range(sub // SUBLANES)]
        for s in range(sub):
            own = s // SUBLANES
            for gi in range(own, sub // SUBLANES):
                rows = slice(gi * SUBLANES, (gi + 1) * SUBLANES)
                rel = b_i[rows] - b_i[s:s + 1]
                if gi == own:
                    rel = jnp.where(tsub >= s - gi * SUBLANES, rel, -jnp.inf)
                w = jnp.sum(q_i[rows] * k_i[s:s + 1] * jnp.exp(rel), axis=1, keepdims=True)
                groups[gi] = jnp.where(acol == r0 + s, w, groups[gi])
        a_i = jnp.concatenate(groups, axis=0)
        if i > 0:
            ref = b[r0 - 1:r0]
            q_t = (q_i * jnp.exp(b_i - ref)).astype(BF16)
            k_t = (k * jnp.exp(jnp.where(tok < r0, ref - b, -jnp.inf))).astype(BF16)
            a_i = a_i + _dot_nt(q_t, k_t)
        blocks.append(a_i)
    attn = jnp.concatenate(blocks, axis=0).astype(BF16)
    o = o + _dot(attn, v)

    k_d = (k * jnp.exp(b_end - b)).astype(BF16)
    st_new = st * jnp.exp(b_end) + _dot_tn(v, k_d)
    return o, st_new


def _gla_body(cl, q_ref, k_ref, v_ref, la_ref, s0_ref, o_ref, sf_ref, st_ref):
    t = pl.program_id(1)
    heads, dv, dk = st_ref.shape

    @pl.when(t == 0)
    def _():
        for hh in range(heads):
            st_ref[hh] = s0_ref[0, hh].T

    n_chunks = q_ref.shape[1] // cl

    def chunk(ci, carry):
        r = pl.multiple_of(ci * cl, cl)
        for hh in range(heads):
            ks = slice(hh * dk, (hh + 1) * dk)
            vs = slice(hh * dv, (hh + 1) * dv)
            o, st = _gla_chunk(q_ref[0, pl.ds(r, cl), ks], k_ref[0, pl.ds(r, cl), ks], v_ref[0, pl.ds(r, cl), vs],
                               la_ref[0, pl.ds(r, cl), ks], st_ref[hh])
            o_ref[0, pl.ds(r, cl), vs] = o.astype(o_ref.dtype)
            st_ref[hh] = st
        return carry

    lax.fori_loop(0, n_chunks, chunk, 0, unroll=2 if n_chunks % 2 == 0 else 1)

    @pl.when(t == pl.num_programs(1) - 1)
    def _():
        for hh in range(heads):
            sf_ref[0, hh] = st_ref[hh].T


def _gla(gq, gk, gv, la, s0, s0_off=0, tt=512):
    b, t, _ = gq.shape
    _, h, dk, dv = s0.shape
    cl = min(GLA_CHUNK, t)
    tt = _row_tile(t, tt)
    return pl.pallas_call(
        functools.partial(_gla_body, cl),
        grid=(b, t // tt),
        in_specs=[pl.BlockSpec((1, tt, h * dk), lambda i, n: (i, n, 0)),
                  pl.BlockSpec((1, tt, h * dk), lambda i, n: (i, n, 0)),
                  pl.BlockSpec((1, tt, h * dv), lambda i, n: (i, n, 0)),
                  pl.BlockSpec((1, tt, h * dk), lambda i, n: (i, n, 0)),
                  pl.BlockSpec((1, h, dk, dv), lambda i, n: (i + s0_off, 0, 0, 0))],
        out_specs=[pl.BlockSpec((1, tt, h * dv), lambda i, n: (i, n, 0)),
                   pl.BlockSpec((1, h, dk, dv), lambda i, n: (i, 0, 0, 0))],
        out_shape=[jax.ShapeDtypeStruct((b, t, h * dv), BF16), jax.ShapeDtypeStruct((b, h, dk, dv), F32)],
        scratch_shapes=[pltpu.VMEM((h, dv, dk), F32)],
        compiler_params=_params(("parallel", "arbitrary")),
        name="gla",
    )(gq, gk, gv, la, s0)


def _head_column(block, head):
    lane = lax.broadcasted_iota(jnp.int32, block.shape, 1)
    return jnp.sum(jnp.where(lane == head, block, 0.0), axis=1, keepdims=True)


def _fox_prompt_body(hps, q_ref, k_ref, v_ref, o_ref, m_ref, acc_ref):
    qi = pl.program_id(2)
    tq = q_ref.shape[1]
    hw = q_ref.shape[2] // hps
    hd = v_ref.shape[2] // hps
    wide = FOX_WIDE * tq

    def logits(hh, r, width):
        sl = slice(hh * hw, (hh + 1) * hw)
        return _dot_nt(q_ref[0, :, sl], k_ref[0, pl.ds(r, width), sl])

    def values(hh, r, width):
        lane = lax.broadcasted_iota(jnp.int32, (width, LANES), 1)
        ones_col = jnp.where(lane == 0, 1.0, 0.0).astype(BF16)
        return jnp.concatenate([v_ref[0, pl.ds(r, width), hh * hd:(hh + 1) * hd], ones_col], axis=1)

    def first(r, width):
        row = lax.broadcasted_iota(jnp.int32, (tq, width), 0)
        col = lax.broadcasted_iota(jnp.int32, (tq, width), 1)
        for hh in range(hps):
            z = jnp.where(col <= row + (width - tq), logits(hh, r, width), -jnp.inf)
            m = jnp.max(z, axis=1, keepdims=True)
            m_ref[hh] = m
            acc_ref[hh] = _dot(jnp.exp2(z - m).astype(BF16), values(hh, r, width))

    assert FOX_WIDE == 2
    lead = (qi + 1) % FOX_WIDE

    @pl.when(lead == 1)
    def _():
        first(pl.multiple_of(qi * tq, tq), tq)

    @pl.when(lead == 0)
    def _():
        first(pl.multiple_of((qi + 1) * tq - wide, tq), wide)

    def update(r, width):
        for hh in range(hps):
            z = logits(hh, r, width)
            m_old = m_ref[hh]
            m_new = jnp.maximum(m_old, jnp.max(z, axis=1, keepdims=True))
            m_ref[hh] = m_new
            pv = _dot(jnp.exp2(z - m_new).astype(BF16), values(hh, r, width))
            acc_ref[hh] = jnp.exp2(m_old - m_new) * acc_ref[hh] + pv

    n_wide = qi // FOX_WIDE

    @pl.when(n_wide % 2 == 1)
    def _():
        update(0, wide)

    base = (n_wide % 2) * wide

    def double_wide_block(j, carry):
        update(pl.multiple_of(base + j * 2 * wide, wide), 2 * wide)
        return carry

    lax.fori_loop(0, n_wide // 2, double_wide_block, 0)
    for hh in range(hps):
        acc = acc_ref[hh]
        o_ref[0, :, hh * hd:(hh + 1) * hd] = (acc[:, :hd] / acc[:, hd:hd + 1]).astype(o_ref.dtype)


def _fox_augment(x, f_cols, query):
    b, t, d = x.shape
    h = f_cols.shape[2]
    hd = d // h

    def top16(v):
        return lax.bitcast_convert_type(lax.bitcast_convert_type(v, jnp.uint32) & jnp.uint32(0xFFFF0000), F32)

    f = f_cols * LOG2E
    hi = top16(f)
    mid = top16(f - hi)
    lo = f - hi - mid
    one = jnp.ones_like(f)
    cols = (hi, mid, lo, one, one, one) if query else (one, one, one, -hi, -mid, -lo)
    lane = lax.broadcasted_iota(jnp.int32, (1, 1, hd), 2)
    parts = []
    for hh in range(h):
        extra = jnp.zeros((b, t, hd), F32)
        for j, c in enumerate(cols):
            extra = jnp.where(lane == j, c[:, :, hh:hh + 1], extra)
        parts += [x[:, :, hh * hd:(hh + 1) * hd], extra.astype(BF16)]
    return jnp.concatenate(parts, axis=-1)


def _fox_prompt(q_aug, k_aug, fv, tq=512, hps=FOX_HEADS_PER_STEP):
    b, t, d = fv.shape
    h = FOX_HEADS
    hd = d // h
    nq = t // tq
    keys = pl.BlockSpec((1, t, hps * 2 * hd), lambda i, j, n: (i, 0, j), pipeline_mode=pl.Buffered(1))
    vals = pl.BlockSpec((1, t, hps * hd), lambda i, j, n: (i, 0, j), pipeline_mode=pl.Buffered(1))
    return pl.pallas_call(
        functools.partial(_fox_prompt_body, hps),
        grid=(b, h // hps, nq),
        in_specs=[pl.BlockSpec((1, tq, hps * 2 * hd), lambda i, j, n: (i, n, j)), keys, vals],
        out_specs=pl.BlockSpec((1, tq, hps * hd), lambda i, j, n: (i, n, j)),
        out_shape=jax.ShapeDtypeStruct((b, t, d), BF16),
        scratch_shapes=[pltpu.VMEM((hps, tq, 1), F32), pltpu.VMEM((hps, tq, hd + LANES), F32)],
        compiler_params=_params(("parallel", "parallel", "arbitrary")),
        name="fox_prompt",
    )(q_aug, k_aug, fv)


def _fox_sample_body(q_ref, kc_ref, vc_ref, kn_ref, vn_ref, fq_ref, fk_ref, o_ref):
    head = pl.program_id(1)
    q = q_ref[0]
    ln = q.shape[0]
    past = kc_ref.shape[1]
    f_q = _head_column(fq_ref[0], head) * LOG2E
    f_k = fk_ref[0, 0] * LOG2E
    z_c = _dot_nt(q, kc_ref[0].astype(BF16)) + f_q - f_k[:, :past]
    z_n = _dot_nt(q, kn_ref[0]) + f_q - f_k[:, past:past + ln]
    row = lax.broadcasted_iota(jnp.int32, z_n.shape, 0)
    col = lax.broadcasted_iota(jnp.int32, z_n.shape, 1)
    z_n = jnp.where(col <= row, z_n, -jnp.inf)
    m = jnp.maximum(jnp.max(z_c, axis=1, keepdims=True), jnp.max(z_n, axis=1, keepdims=True))
    p_c = jnp.exp2(z_c - m)
    p_n = jnp.exp2(z_n - m)
    denom = jnp.sum(p_c, axis=1, keepdims=True) + jnp.sum(p_n, axis=1, keepdims=True)
    o = _dot(p_c.astype(BF16), vc_ref[0].astype(BF16)) + _dot(p_n.astype(BF16), vn_ref[0])
    o_ref[0] = (o / denom).astype(o_ref.dtype)


def _fox_sample(fq, fk, fv, cache_k, cache_v, cache_off, f_cols, f_rows):
    b, ln, d = fq.shape
    past = cache_k.shape[1]
    h = f_cols.shape[2]
    hd = d // h
    fw = f_rows.shape[3]
    return pl.pallas_call(
        _fox_sample_body,
        grid=(b, h),
        in_specs=[pl.BlockSpec((1, ln, hd), lambda i, j: (i, 0, j)),
                  pl.BlockSpec((1, past, hd), lambda i, j: (i + cache_off, 0, j)),
                  pl.BlockSpec((1, past, hd), lambda i, j: (i + cache_off, 0, j)),
                  pl.BlockSpec((1, ln, hd), lambda i, j: (i, 0, j)),
                  pl.BlockSpec((1, ln, hd), lambda i, j: (i, 0, j)),
                  pl.BlockSpec((1, ln, h), lambda i, j: (i, 0, 0)),
                  pl.BlockSpec((1, 1, 1, fw), lambda i, j: (i, j, 0, 0))],
        out_specs=pl.BlockSpec((1, ln, hd), lambda i, j: (i, 0, j)),
        out_shape=jax.ShapeDtypeStruct((b, ln, d), BF16),
        compiler_params=_params(("parallel", "parallel")),
        name="fox_sample",
    )(fq, cache_k, cache_v, fk, fv, f_cols, f_rows)


def _mem_kv_body(mem_ref, g_ref, w_ref, k_ref, v_ref):
    u = _rmsnorm(mem_ref[0], g_ref[...]).astype(BF16)
    d = k_ref.shape[2]
    k_ref[0] = _dot(u, w_ref[:, :d])
    v_ref[0] = _dot(u, w_ref[:, d:])


def _mem_kv(mem, g, w):
    b, n, d = mem.shape
    blk = pl.BlockSpec((1, n, d), lambda i: (i, 0, 0))
    return pl.pallas_call(
        _mem_kv_body,
        grid=(b,),
        in_specs=[blk, _resident(g.shape), _resident(w.shape)],
        out_specs=[blk, blk],
        out_shape=[jax.ShapeDtypeStruct((b, n, d), F32)] * 2,
        compiler_params=_params(("parallel",)),
        name="mem_kv",
    )(mem, g, w)


def _mix_out_body(seq, x_ref, og_ref, of_ref, mk_ref, mv_ref, pre_ref, w2_ref, gn_ref, wgo_ref, wfo_ref, wmo_ref,
                  wout_ref, post_ref, o_ref, u_ref):
    x = x_ref[...]
    tm, d = x.shape
    u_ref[...] = _rmsnorm(x, pre_ref[...]).astype(BF16)

    def proj(j):
        return _dot(u_ref[...], w2_ref[:, j * d:(j + 1) * d])

    mq = proj(1)
    hd = d // MEM_HEADS
    scale = float(hd ** -0.5)
    per_batch = []
    for j in range(tm // seq):
        mk = mk_ref[j].astype(BF16)
        mv = mv_ref[j].astype(BF16)
        heads = []
        for hh in range(MEM_HEADS):
            sl = slice(hh * hd, (hh + 1) * hd)
            s = _dot_nt(mq[j * seq:(j + 1) * seq, sl].astype(BF16), mk[:, sl]) * scale
            e = jnp.exp(s - jnp.max(s, axis=1, keepdims=True))
            p = e / jnp.sum(e, axis=1, keepdims=True)
            heads.append(_dot(p.astype(BF16), mv[:, sl]))
        per_batch.append(jnp.concatenate(heads, axis=1))
    o_mem = per_batch[0] if len(per_batch) == 1 else jnp.concatenate(per_batch, axis=0)
    y = _sigmoid(proj(4)) * _dot(o_mem.astype(BF16), wmo_ref[...])

    y = y + _sigmoid(proj(3)) * _dot(of_ref[...], wfo_ref[...])

    og = og_ref[...].astype(F32)
    gd = d // GLA_HEADS
    gn = gn_ref[...]
    normed = []
    for hh in range(GLA_HEADS):
        sl = slice(hh * gd, (hh + 1) * gd)
        normed.append(_rmsnorm(og[:, sl], gn[:, sl]))
    r = proj(0)
    b_gla = _dot((jnp.concatenate(normed, axis=1) * (r * _sigmoid(r))).astype(BF16), wgo_ref[...])
    y = y + _sigmoid(proj(2)) * b_gla

    mixed = _dot(y.astype(BF16), wout_ref[...])
    o_ref[...] = x + _rmsnorm(mixed, post_ref[...])


def _mix_out(x2, o_gla, o_fox, mk, mv, mem_off, seq, pre_g, w2, gn, wgo, wfo, wmo, wout, post_g, tm=512):
    m, d = x2.shape
    if seq < tm:
        nb = 4
        while (m // seq) % nb:
            nb //= 2
        tm = seq * nb
    assert m % tm == 0 and (seq % tm == 0 or tm % seq == 0)
    nb = max(1, tm // seq)
    steps_per_batch = max(1, seq // tm)
    mem = mk.shape[1]
    rows = pl.BlockSpec((tm, d), lambda i: (i, 0))
    assert mem_off % nb == 0
    memb = pl.BlockSpec((nb, mem, d), lambda i: (i // steps_per_batch + mem_off // nb, 0, 0))
    return pl.pallas_call(
        functools.partial(_mix_out_body, min(seq, tm)),
        grid=(m // tm,),
        in_specs=[rows, rows, rows, memb, memb, _resident(pre_g.shape), _resident(w2.shape), _resident(gn.shape),
                  _resident(wgo.shape), _resident(wfo.shape), _resident(wmo.shape), _resident(wout.shape),
                  _resident(post_g.shape)],
        out_specs=rows,
        out_shape=jax.ShapeDtypeStruct((m, d), F32),
        scratch_shapes=[pltpu.VMEM((tm, d), BF16)],
        compiler_params=_params(("parallel",)),
        name="mix_out",
    )(x2, o_gla, o_fox, mk, mv, pre_g, w2, gn, wgo, wfo, wmo, wout, post_g)


def _prep_layer(d, w):
    gk_w = d // 2
    sizes = (gk_w, gk_w, d, GLA_RANK, d, d, d, d, FOX_HEADS, d, N_BRANCH * d)
    offs = np.concatenate([[0], np.cumsum(sizes)]).tolist()
    seg = {n: (offs[i], offs[i + 1]) for i, n in enumerate(
        ("gq", "gk", "gv", "glr", "gr", "fq", "fk", "fv", "ff", "mq", "gates"))}
    w_in = w["w_in"]

    def cols(*names):
        return jnp.concatenate([w_in[:, seg[n][0]:seg[n][1]] for n in names], axis=1).astype(BF16)

    ws = jnp.pad(cols("ff", "glr"), ((0, 0), (0, SMALL_W - FOX_HEADS - GLA_RANK)))
    wa2p = jnp.pad(w["gla_w_a2"].astype(BF16), ((FOX_HEADS, SMALL_W - FOX_HEADS - GLA_RANK), (0, 0)))
    b_f = jnp.pad(w["fox_b_f"], (0, SMALL_W - FOX_HEADS)).reshape(1, SMALL_W)

    def ffn(pfx):
        wg, wu, wd = w[pfx + "_w_gate"], w[pfx + "_w_up"], w[pfx + "_w_down"]
        assert wg.shape[1] % FFN_CHUNK == 0
        return dict(pre=w[pfx + "_pre_g"].reshape(1, d), post=w[pfx + "_post_g"].reshape(1, d),
                    wg=wg.astype(BF16), wu=wu.astype(BF16), wd=wd.astype(BF16))

    return dict(
        ffn1=ffn("ffn1"), ffn2=ffn("ffn2"),
        mix_pre=w["mix_pre_g"].reshape(1, d), mix_post=w["mix_post_g"].reshape(1, d),
        w1=cols("gq", "gk", "gv", "fq", "fk", "fv"), ws=ws, wa2p=wa2p,
        b_a=w["gla_b_a"].reshape(1, gk_w), b_f=b_f,
        w2=cols("gr", "mq", "gates"),
        gn=w["gla_norm_g"].reshape(1, d),
        wgo=w["w_gla_o"].astype(BF16), wfo=w["w_fox_o"].astype(BF16), wmo=w["w_mem_o"].astype(BF16),
        wout=w["w_out"].astype(BF16),
        mem_g=w["mem_norm_g"].reshape(1, d), w_mem_kv=w["w_mem_kv"].astype(BF16),
    )


def _ffn_apply(x2, p):
    return _ffn(x2, p["pre"], p["post"], p["wg"], p["wu"], p["wd"])


def _forget_layouts(f, lead=None):
    b, t, h = f.shape
    if lead is not None:
        f = jnp.concatenate([lead.astype(F32), f], axis=1)
    total = f.shape[1]
    padded = -(-total // LANES) * LANES
    f = jnp.pad(f, ((0, 0), (0, padded - total), (0, 0)))
    rows = _cumsum_time(jnp.swapaxes(f, 1, 2).reshape(b * h, padded // LANES, LANES))
    rows = rows.reshape(b, h, padded)
    cols = jnp.swapaxes(rows[:, :, total - t:total], 1, 2)
    return cols, rows


def _group_layer(x2, b, t, p, layer, depth, carried, gla_state, fox, memory):
    d = x2.shape[1]
    x2 = _ffn_apply(x2, p["ffn1"])
    gq, gk, gv, la, fq, fk16, fv16, *carried = _mix_in(x2, p["mix_pre"], p["w1"], p["ws"], p["wa2p"], p["b_a"],
                                                       p["b_f"], layer, depth, carried)
    r3 = lambda a: a.reshape(b, t, a.shape[1])
    o_gla, s_fin = _gla(r3(gq), r3(gk), r3(gv), r3(la), gla_state[0], gla_state[1])
    f_log = carried[2][layer].reshape(b, t, SMALL_W)[:, :, :FOX_HEADS]
    o_fox = fox(r3(fq), r3(fk16), r3(fv16), f_log)
    x2 = _mix_out(x2, o_gla.reshape(b * t, d), o_fox.reshape(b * t, d), memory[0], memory[1], memory[2], t,
                  p["mix_pre"], p["w2"], p["gn"], p["wgo"], p["wfo"], p["wmo"], p["wout"], p["mix_post"])
    x2 = _ffn_apply(x2, p["ffn2"])
    return x2, carried, s_fin


def kernel(x_prompt, x_sample, cache_fox_k, cache_fox_v, cache_fox_logf, state_gla, cache_mem_k, cache_mem_v, mem_prompt, ffn1_pre_g, ffn1_post_g, ffn1_w_gate, ffn1_w_up, ffn1_w_down, mix_pre_g, mix_post_g, w_in, gla_w_a2, gla_b_a, fox_b_f, gla_norm_g, w_gla_o, w_fox_o, w_mem_o, w_out, mem_norm_g, w_mem_kv, ffn2_pre_g, ffn2_post_g, ffn2_w_gate, ffn2_w_up, ffn2_w_down):
    weights = dict(ffn1_pre_g=ffn1_pre_g, ffn1_post_g=ffn1_post_g, ffn1_w_gate=ffn1_w_gate, ffn1_w_up=ffn1_w_up,
                   ffn1_w_down=ffn1_w_down, mix_pre_g=mix_pre_g, mix_post_g=mix_post_g, w_in=w_in,
                   gla_w_a2=gla_w_a2, gla_b_a=gla_b_a, fox_b_f=fox_b_f, gla_norm_g=gla_norm_g, w_gla_o=w_gla_o,
                   w_fox_o=w_fox_o, w_mem_o=w_mem_o, w_out=w_out, mem_norm_g=mem_norm_g, w_mem_kv=w_mem_kv,
                   ffn2_pre_g=ffn2_pre_g, ffn2_post_g=ffn2_post_g, ffn2_w_gate=ffn2_w_gate, ffn2_w_up=ffn2_w_up,
                   ffn2_w_down=ffn2_w_down)
    depth = w_in.shape[0]
    bp, tp, d = x_prompt.shape
    bs, ts, _ = x_sample.shape
    hd = d // FOX_HEADS
    mh = d // MEM_HEADS
    dk, dv = state_gla.shape[-2:]
    layers = [_prep_layer(d, {k: v[l] for k, v in weights.items()}) for l in range(depth)]

    x = x_prompt.reshape(bp * tp, d)
    zero_state = jnp.zeros((bp, GLA_HEADS, dk, dv), F32)
    tq = min(512, tp)

    def fox_p(fq, fk, fv, f_log):
        f_cols, _ = _forget_layouts(f_log)
        return _fox_prompt(_fox_augment(fq, f_cols, True), _fox_augment(fk, f_cols, False), fv, tq=tq)

    carried, p_state, p_mk, p_mv = [], [], [], []
    for l in range(depth):
        mk, mv = _mem_kv(mem_prompt, layers[l]["mem_g"], layers[l]["w_mem_kv"])
        x, carried, s_fin = _group_layer(x, bp, tp, layers[l], l, depth, carried, (zero_state, 0), fox_p, (mk, mv, 0))
        p_state.append(s_fin)
        p_mk.append(mk)
        p_mv.append(mv)
    y_prompt = x.reshape(bp, tp, d)
    p_fox_k = carried[0].reshape(depth, bp, tp, FOX_HEADS, hd)
    p_fox_v = carried[1].reshape(depth, bp, tp, FOX_HEADS, hd)
    p_fox_logf = carried[2].reshape(depth, bp, tp, SMALL_W)[..., :FOX_HEADS]
    mem_len = mem_prompt.shape[1]
    p_mem_k = jnp.stack(p_mk).reshape(depth, bp, mem_len, MEM_HEADS, mh)
    p_mem_v = jnp.stack(p_mv).reshape(depth, bp, mem_len, MEM_HEADS, mh)

    x = x_sample.reshape(bs * ts, d)
    past = cache_fox_k.shape[2]
    ck = cache_fox_k.reshape(depth * bs, past, d)
    cv = cache_fox_v.reshape(depth * bs, past, d)
    states = state_gla.reshape(depth * bs, GLA_HEADS, dk, dv)
    mem_s = cache_mem_k.shape[2]
    cmk = cache_mem_k.reshape(depth * bs, mem_s, d)
    cmv = cache_mem_v.reshape(depth * bs, mem_s, d)
    carried, s_state = [], []
    for l in range(depth):
        def fox_s(fq, fk, fv, f_log, l=l):
            f_cols, f_rows = _forget_layouts(f_log, lead=cache_fox_logf[l])
            return _fox_sample(fq, fk, fv, ck, cv, l * bs, f_cols, f_rows[:, :, None, :])

        x, carried, s_new = _group_layer(x, bs, ts, layers[l], l, depth, carried, (states, l * bs), fox_s,
                                         (cmk, cmv, l * bs))
        s_state.append(s_new)
    y_sample = x.reshape(bs, ts, d)
    s_fox_k = carried[0].reshape(depth, bs, ts, FOX_HEADS, hd)
    s_fox_v = carried[1].reshape(depth, bs, ts, FOX_HEADS, hd)
    s_fox_logf = carried[2].reshape(depth, bs, ts, SMALL_W)[..., :FOX_HEADS]

    return (y_prompt, y_sample, p_fox_k, p_fox_v, p_fox_logf, jnp.stack(p_state), p_mem_k, p_mem_v,
            s_fox_k, s_fox_v, s_fox_logf, jnp.stack(s_state))
```

```python
import functools

import numpy as np
import jax
import jax.numpy as jnp
from jax import lax
from jax.experimental import pallas as pl
from jax.experimental.pallas import tpu as pltpu

F32 = jnp.float32
BF16 = jnp.bfloat16

EPS = 1e-6
GLA_HEADS = 4
GLA_RANK = 16
GLA_TAU = 16.0
FOX_HEADS = 8
MEM_HEADS = 4
N_BRANCH = 3

LANES = 128
SUBLANES = 8
V7X_VMEM_LIMIT = 56 * 1024 * 1024
FFN_CHUNK = 256
GLA_CHUNK = 64
GLA_SUB = 16
FOX_HEADS_PER_STEP = 4
FOX_WIDE = 2
LOG2E = 1.4426950408889634
SMALL_W = 128


def _dot(a, b):
    return jnp.dot(a, b, preferred_element_type=F32)


def _dot_nt(a, b):
    return lax.dot_general(a, b, (((1,), (1,)), ((), ())), preferred_element_type=F32)


def _dot_tn(a, b):
    return lax.dot_general(a, b, (((0,), (0,)), ((), ())), preferred_element_type=F32)


def _rmsnorm(x, g):
    return x * lax.rsqrt(jnp.mean(x * x, axis=-1, keepdims=True) + EPS) * g


def _sigmoid(x):
    return 1.0 / (1.0 + jnp.exp(-x))


def _log_sigmoid(x):
    return jnp.minimum(x, 0.0) - jnp.log1p(jnp.exp(-jnp.abs(x)))


def _split3(x):
    hi = x.astype(BF16)
    r1 = x - hi.astype(F32)
    mid = r1.astype(BF16)
    lo = (r1 - mid.astype(F32)).astype(BF16)
    return hi, mid, lo


def _resident(shape):
    nd = len(shape)
    return pl.BlockSpec(shape, lambda *_: (0,) * nd, pipeline_mode=pl.Buffered(1))


def _params(semantics):
    return pltpu.CompilerParams(dimension_semantics=semantics, vmem_limit_bytes=V7X_VMEM_LIMIT)


def _row_tile(m, want):
    t = min(m, want)
    assert m % t == 0, (m, t)
    return t


def _ffn_body(x_ref, pre_ref, post_ref, wg_ref, wu_ref, wd_ref, o_ref, u_ref, acc_ref):
    x = x_ref[...]
    u_ref[...] = _rmsnorm(x, pre_ref[...]).astype(BF16)
    for c in range(wg_ref.shape[1] // FFN_CHUNK):
        cols = slice(c * FFN_CHUNK, (c + 1) * FFN_CHUNK)
        u = u_ref[...]
        g = _dot(u, wg_ref[:, cols])
        up = _dot(u, wu_ref[:, cols])
        h = (g * _sigmoid(g) * up).astype(BF16)
        part = _dot(h, wd_ref[cols, :])
        if c == 0:
            acc_ref[...] = part
        else:
            acc_ref[...] += part
    o_ref[...] = x + 0.5 * _rmsnorm(acc_ref[...], post_ref[...])


def _ffn(x2, pre_g, post_g, wg, wu, wd, tm=512):
    m, d = x2.shape
    tm = _row_tile(m, tm)
    return pl.pallas_call(
        _ffn_body,
        grid=(m // tm,),
        in_specs=[pl.BlockSpec((tm, d), lambda i: (i, 0)),
                  _resident(pre_g.shape), _resident(post_g.shape),
                  _resident(wg.shape), _resident(wu.shape), _resident(wd.shape)],
        out_specs=pl.BlockSpec((tm, d), lambda i: (i, 0)),
        out_shape=jax.ShapeDtypeStruct((m, d), F32),
        scratch_shapes=[pltpu.VMEM((tm, d), BF16), pltpu.VMEM((tm, d), F32)],
        compiler_params=_params(("parallel",)),
        name="ffn",
    )(x2, pre_g, post_g, wg, wu, wd)


def _mix_in_body(dims, n_carried, x_ref, pre_ref, w_ref, ws_ref, wa2_ref, ba_ref, bf_ref, *refs):
    gq_ref, gk_ref, gv_ref, la_ref, fq_ref, fk16_ref, fv16_ref, fk32_ref, fv32_ref, fl_ref, u_ref = refs[n_carried:]
    gk_w, gv_w, fx_w, dk_scale, hd_scale = dims
    u_ref[...] = _rmsnorm(x_ref[...], pre_ref[...]).astype(BF16)

    def proj(lo, width):
        return _dot(u_ref[...], w_ref[:, lo:lo + width])

    off = 0
    gq_ref[...] = proj(off, gk_w) * dk_scale
    off += gk_w
    gk_ref[...] = proj(off, gk_w)
    off += gk_w
    gv_ref[...] = proj(off, gv_w).astype(BF16)
    off += gv_w
    fq_ref[...] = (proj(off, fx_w) * hd_scale).astype(BF16)
    off += fx_w
    hd = fx_w // FOX_HEADS
    fk = proj(off, fx_w)
    fk16_ref[...] = fk.astype(BF16)
    off += fx_w
    fv = proj(off, fx_w)
    fv16_ref[...] = fv.astype(BF16)
    for hh in range(FOX_HEADS):
        fk32_ref[:, hh, :] = fk[:, hh * hd:(hh + 1) * hd]
        fv32_ref[:, hh, :] = fv[:, hh * hd:(hh + 1) * hd]

    small = _dot(u_ref[...], ws_ref[...])
    lane = lax.broadcasted_iota(jnp.int32, small.shape, 1)
    fl_ref[...] = jnp.where(lane < FOX_HEADS, _log_sigmoid(small + bf_ref[...]), 0.0)
    lr = jnp.where((lane >= FOX_HEADS) & (lane < FOX_HEADS + GLA_RANK), small, 0.0).astype(BF16)
    la_ref[...] = _log_sigmoid(_dot(lr, wa2_ref[...]) + ba_ref[...]) * (1.0 / GLA_TAU)


def _mix_in(x2, pre_g, w1, ws, wa2p, b_a, b_f, layer, depth, carried, tm=512):
    m, d = x2.shape
    tm = _row_tile(m, tm)
    gk_w = d // 2
    dims = (gk_w, d, d, float((gk_w // GLA_HEADS) ** -0.5), float((d // FOX_HEADS) ** -0.5) * LOG2E)

    def rows(width):
        return pl.BlockSpec((tm, width), lambda i: (i, 0))

    def stacked(*tail):
        return pl.BlockSpec((None, tm) + tail, lambda i: (layer, i) + (0,) * len(tail))

    outs = [(gk_w, F32), (gk_w, F32), (d, BF16), (gk_w, F32), (d, BF16), (d, BF16), (d, BF16)]
    stacks = [(FOX_HEADS, d // FOX_HEADS), (FOX_HEADS, d // FOX_HEADS), (SMALL_W,)]
    n_in = 7
    return pl.pallas_call(
        functools.partial(_mix_in_body, dims, len(carried)),
        grid=(m // tm,),
        in_specs=[rows(d), _resident(pre_g.shape), _resident(w1.shape), _resident(ws.shape),
                  _resident(wa2p.shape), _resident(b_a.shape), _resident(b_f.shape)]
                 + [pl.BlockSpec(memory_space=pl.ANY)] * len(carried),
        out_specs=[rows(w) for w, _ in outs] + [stacked(*tail) for tail in stacks],
        out_shape=[jax.ShapeDtypeStruct((m, w), dt) for w, dt in outs]
                  + [jax.ShapeDtypeStruct((depth, m) + tail, F32) for tail in stacks],
        input_output_aliases={n_in + j: len(outs) + j for j in range(len(carried))},
        scratch_shapes=[pltpu.VMEM((tm, d), BF16)],
        compiler_params=_params(("parallel",)),
        name="mix_in",
    )(x2, pre_g, w1, ws, wa2p, b_a, b_f, *carried)


def _cumsum_body(x_ref, m_ref, o_ref):
    x = x_ref[...]
    r = lax.broadcasted_iota(jnp.int32, (LANES, LANES), 0)
    c = lax.broadcasted_iota(jnp.int32, (LANES, LANES), 1)
    upper = jnp.where(r <= c, 1.0, 0.0).astype(BF16)
    hi, mid, lo = _split3(x)
    o_ref[...] = _dot(hi, upper) + _dot(mid, upper) + _dot(lo, upper)
    tot = jnp.broadcast_to(o_ref[:, LANES - 1:LANES], x.shape)
    hi, mid, lo = _split3(tot)
    mm = m_ref[...]
    offset = _dot(mm, hi) + _dot(mm, mid) + _dot(mm, lo)
    o_ref[...] = o_ref[...] + offset


def _cumsum_time(x3):
    g, n, _ = x3.shape
    per = 8
    while (per * n) % 8:
        per += 1
    gpb = per if g % per == 0 else g
    rb = gpb * n
    idx = np.arange(rb)
    earlier = (idx[:, None] // n == idx[None, :] // n) & (idx[None, :] < idx[:, None])
    mm = jnp.asarray(earlier.astype(np.float32), BF16)
    out = pl.pallas_call(
        _cumsum_body,
        grid=(g // gpb,),
        in_specs=[pl.BlockSpec((rb, LANES), lambda i: (i, 0)), _resident((rb, rb))],
        out_specs=pl.BlockSpec((rb, LANES), lambda i: (i, 0)),
        out_shape=jax.ShapeDtypeStruct((g * n, LANES), F32),
        compiler_params=_params(("parallel",)),
        name="cumsum",
    )(x3.reshape(g * n, LANES), mm)
    return out.reshape(g, n, LANES)


def _gla_chunk(q, k, v, la, st):
    cl, dk = q.shape
    sub = min(GLA_SUB, cl)
    row = lax.broadcasted_iota(jnp.int32, (cl, cl), 0)
    col = lax.broadcasted_iota(jnp.int32, (cl, cl), 1)
    lower = jnp.where(row >= col, 1.0, 0.0).astype(BF16)
    hi, mid, lo = _split3(la)
    b = _dot(lower, hi) + _dot(lower, mid) + _dot(lower, lo)
    b_end = b[cl - 1:cl, :]

    o = _dot_nt((q * jnp.exp(b)).astype(BF16), st.astype(BF16))

    tsub = lax.broadcasted_iota(jnp.int32, (SUBLANES, dk), 0)
    acol = lax.broadcasted_iota(jnp.int32, (SUBLANES, cl), 1)
    blocks = []
    for i in range(cl // sub):
        r0 = i * sub
        b_i, q_i, k_i = b[r0:r0 + sub], q[r0:r0 + sub], k[r0:r0 + sub]
        groups = [jnp.zeros((SUBLANES, cl), F32) for _ in range(sub // SUBLANES)]
        for s in range(sub):
            own = s // SUBLANES
            for gi in range(own, sub // SUBLANES):
                rows = slice(gi * SUBLANES, (gi + 1) * SUBLANES)
                rel = b_i[rows] - b_i[s:s + 1]
                if gi == own:
                    rel = jnp.where(tsub >= s - gi * SUBLANES, rel, -jnp.inf)
                w = jnp.sum(q_i[rows] * k_i[s:s + 1] * jnp.exp(rel), axis=1, keepdims=True)
                groups[gi] = jnp.where(acol == r0 + s, w, groups[gi])
        a_i = jnp.concatenate(groups, axis=0)
        if i > 0:
            ref = b[r0 - 1:r0]
            q_t = (q_i * jnp.exp(b_i - ref)).astype(BF16)
            k_t = jnp.concatenate([k[:r0] * jnp.exp(ref - b[:r0]), jnp.zeros((cl - r0, dk), F32)], axis=0)
            a_i = a_i + _dot_nt(q_t, k_t.astype(BF16))
        blocks.append(a_i)
    attn = jnp.concatenate(blocks, axis=0).astype(BF16)
    o = o + _dot(attn, v)

    k_d = (k * jnp.exp(b_end - b)).astype(BF16)
    st_new = st * jnp.exp(b_end) + _dot_tn(v, k_d)
    return o, st_new


def _gla_body(cl, q_ref, k_ref, v_ref, la_ref, s0_ref, o_ref, sf_ref, st_ref):
    t = pl.program_id(1)
    heads, dv, dk = st_ref.shape

    @pl.when(t == 0)
    def _():
        for hh in range(heads):
            st_ref[hh] = s0_ref[0, hh].T

    n_chunks = q_ref.shape[1] // cl

    def chunk(ci, carry):
        r = pl.multiple_of(ci * cl, cl)
        for hh in range(heads):
            ks = slice(hh * dk, (hh + 1) * dk)
            vs = slice(hh * dv, (hh + 1) * dv)
            o, st = _gla_chunk(q_ref[0, pl.ds(r, cl), ks], k_ref[0, pl.ds(r, cl), ks], v_ref[0, pl.ds(r, cl), vs],
                               la_ref[0, pl.ds(r, cl), ks], st_ref[hh])
            o_ref[0, pl.ds(r, cl), vs] = o.astype(o_ref.dtype)
            st_ref[hh] = st
        return carry

    lax.fori_loop(0, n_chunks, chunk, 0, unroll=2 if n_chunks % 2 == 0 else 1)

    @pl.when(t == pl.num_programs(1) - 1)
    def _():
        for hh in range(heads):
            sf_ref[0, hh] = st_ref[hh].T


def _gla(gq, gk, gv, la, s0, s0_off=0, tt=512):
    b, t, _ = gq.shape
    _, h, dk, dv = s0.shape
    cl = min(GLA_CHUNK, t)
    tt = _row_tile(t, tt)
    return pl.pallas_call(
        functools.partial(_gla_body, cl),
        grid=(b, t // tt),
        in_specs=[pl.BlockSpec((1, tt, h * dk), lambda i, n: (i, n, 0)),
                  pl.BlockSpec((1, tt, h * dk), lambda i, n: (i, n, 0)),
                  pl.BlockSpec((1, tt, h * dv), lambda i, n: (i, n, 0)),
                  pl.BlockSpec((1, tt, h * dk), lambda i, n: (i, n, 0)),
                  pl.BlockSpec((1, h, dk, dv), lambda i, n: (i + s0_off, 0, 0, 0))],
        out_specs=[pl.BlockSpec((1, tt, h * dv), lambda i, n: (i, n, 0)),
                   pl.BlockSpec((1, h, dk, dv), lambda i, n: (i, 0, 0, 0))],
        out_shape=[jax.ShapeDtypeStruct((b, t, h * dv), BF16), jax.ShapeDtypeStruct((b, h, dk, dv), F32)],
        scratch_shapes=[pltpu.VMEM((h, dv, dk), F32)],
        compiler_params=_params(("parallel", "arbitrary")),
        name="gla",
    )(gq, gk, gv, la, s0)


def _fox_prompt_body(hps, q_ref, k_ref, v_ref, o_ref, m_ref, acc_ref):
    qi = pl.program_id(2)
    tq = q_ref.shape[1]
    hw = q_ref.shape[2] // hps
    hd = v_ref.shape[2] // hps
    wide = FOX_WIDE * tq

    def logits(hh, r, width):
        sl = slice(hh * hw, (hh + 1) * hw)
        return _dot_nt(q_ref[0, :, sl], k_ref[0, pl.ds(r, width), sl])

    def values(hh, r, width):
        lane = lax.broadcasted_iota(jnp.int32, (width, LANES), 1)
        ones_col = jnp.where(lane == 0, 1.0, 0.0).astype(BF16)
        return jnp.concatenate([v_ref[0, pl.ds(r, width), hh * hd:(hh + 1) * hd], ones_col], axis=1)

    def first(r, width):
        row = lax.broadcasted_iota(jnp.int32, (tq, width), 0)
        col = lax.broadcasted_iota(jnp.int32, (tq, width), 1)
        for hh in range(hps):
            z = jnp.where(col <= row + (width - tq), logits(hh, r, width), -jnp.inf)
            m = jnp.max(z, axis=1, keepdims=True)
            m_ref[hh] = m
            acc_ref[hh] = _dot(jnp.exp2(z - m).astype(BF16), values(hh, r, width))

    assert FOX_WIDE == 2
    lead = (qi + 1) % FOX_WIDE

    @pl.when(lead == 1)
    def _():
        first(pl.multiple_of(qi * tq, tq), tq)

    @pl.when(lead == 0)
    def _():
        first(pl.multiple_of((qi + 1) * tq - wide, tq), wide)

    def update(r, width):
        for hh in range(hps):
            z = logits(hh, r, width)
            m_old = m_ref[hh]
            m_new = jnp.maximum(m_old, jnp.max(z, axis=1, keepdims=True))
            m_ref[hh] = m_new
            pv = _dot(jnp.exp2(z - m_new).astype(BF16), values(hh, r, width))
            acc_ref[hh] = jnp.exp2(m_old - m_new) * acc_ref[hh] + pv

    n_wide = qi // FOX_WIDE

    @pl.when(n_wide % 2 == 1)
    def _():
        update(0, wide)

    base = (n_wide % 2) * wide

    def double_wide_block(j, carry):
        update(pl.multiple_of(base + j * 2 * wide, wide), 2 * wide)
        return carry

    lax.fori_loop(0, n_wide // 2, double_wide_block, 0)
    for hh in range(hps):
        acc = acc_ref[hh]
        o_ref[0, :, hh * hd:(hh + 1) * hd] = (acc[:, :hd] / acc[:, hd:hd + 1]).astype(o_ref.dtype)


def _fox_augment(x, f_cols, query):
    b, t, d = x.shape
    h = f_cols.shape[2]
    hd = d // h

    def top16(v):
        return lax.bitcast_convert_type(lax.bitcast_convert_type(v, jnp.uint32) & jnp.uint32(0xFFFF0000), F32)

    f = f_cols * LOG2E
    hi = top16(f)
    mid = top16(f - hi)
    lo = f - hi - mid
    one = jnp.ones_like(f)
    cols = (hi, mid, lo, one, one, one) if query else (one, one, one, -hi, -mid, -lo)
    lane = lax.broadcasted_iota(jnp.int32, (1, 1, hd), 2)
    parts = []
    for hh in range(h):
        extra = jnp.zeros((b, t, hd), F32)
        for j, c in enumerate(cols):
            extra = jnp.where(lane == j, c[:, :, hh:hh + 1], extra)
        parts += [x[:, :, hh * hd:(hh + 1) * hd], extra.astype(BF16)]
    return jnp.concatenate(parts, axis=-1)


def _fox_prompt(q_aug, k_aug, fv, tq=512, hps=FOX_HEADS_PER_STEP):
    b, t, d = fv.shape
    h = FOX_HEADS
    hd = d // h
    nq = t // tq
    keys = pl.BlockSpec((1, t, hps * 2 * hd), lambda i, j, n: (i, 0, j), pipeline_mode=pl.Buffered(1))
    vals = pl.BlockSpec((1, t, hps * hd), lambda i, j, n: (i, 0, j), pipeline_mode=pl.Buffered(1))
    return pl.pallas_call(
        functools.partial(_fox_prompt_body, hps),
        grid=(b, h // hps, nq),
        in_specs=[pl.BlockSpec((1, tq, hps * 2 * hd), lambda i, j, n: (i, n, j)), keys, vals],
        out_specs=pl.BlockSpec((1, tq, hps * hd), lambda i, j, n: (i, n, j)),
        out_shape=jax.ShapeDtypeStruct((b, t, d), BF16),
        scratch_shapes=[pltpu.VMEM((hps, tq, 1), F32), pltpu.VMEM((hps, tq, hd + LANES), F32)],
        compiler_params=_params(("parallel", "parallel", "arbitrary")),
        name="fox_prompt",
    )(q_aug, k_aug, fv)


def _fox_sample_body(heads, q_ref, kc_ref, vc_ref, kn_ref, vn_ref, fq_ref, fkc_ref, fkn_ref, o_ref,
                     m_ref, l_ref, acc_ref):
    c = pl.program_id(1)
    q = q_ref[0]
    rows = q.shape[0]
    ln = rows // heads
    f_q = fq_ref[0] * LOG2E
    head_q = lax.broadcasted_iota(jnp.int32, (rows, 1), 0) // ln

    def head_k(n):
        return lax.broadcasted_iota(jnp.int32, (1, n), 1) % heads

    @pl.when(c == 0)
    def _():
        n = kn_ref.shape[1]
        z = _dot_nt(q, kn_ref[0]) + f_q - fkn_ref[0] * LOG2E
        step_k = lax.broadcasted_iota(jnp.int32, (1, n), 1) // heads
        step_q = lax.broadcasted_iota(jnp.int32, (rows, 1), 0) % ln
        z = jnp.where((head_k(n) == head_q) & (step_k <= step_q), z, -jnp.inf)
        m = jnp.max(z, axis=1, keepdims=True)
        p = jnp.exp2(z - m)
        m_ref[...] = m
        l_ref[...] = jnp.sum(p, axis=1, keepdims=True)
        acc_ref[...] = _dot(p.astype(BF16), vn_ref[0])

    n = kc_ref.shape[1]
    z = _dot_nt(q, kc_ref[0].astype(BF16)) + f_q - fkc_ref[0] * LOG2E
    z = jnp.where(head_k(n) == head_q, z, -jnp.inf)
    m_old = m_ref[...]
    m_new = jnp.maximum(m_old, jnp.max(z, axis=1, keepdims=True))
    alpha = jnp.exp2(m_old - m_new)
    p = jnp.exp2(z - m_new)
    m_ref[...] = m_new
    l_ref[...] = alpha * l_ref[...] + jnp.sum(p, axis=1, keepdims=True)
    acc_ref[...] = alpha * acc_ref[...] + _dot(p.astype(BF16), vc_ref[0].astype(BF16))

    @pl.when(c == pl.num_programs(1) - 1)
    def _():
        o_ref[0] = (acc_ref[...] / l_ref[...]).astype(o_ref.dtype)


def _fox_sample(fq, fk, fv, cache_k, cache_v, cache_off, f_all, chunk_steps=512):
    b, ln, d = fq.shape
    rows_total, past, h, hd = cache_k.shape
    assert h & (h - 1) == 0 and ln & (ln - 1) == 0
    chunk_steps = _row_tile(past, chunk_steps)
    ch = chunk_steps * h
    q_rows = fq.reshape(b, ln, h, hd).transpose(0, 2, 1, 3).reshape(b, h * ln, hd)
    f_q = jnp.swapaxes(f_all[:, past:, :], 1, 2).reshape(b, h * ln, 1)
    f_k = f_all.reshape(b, 1, (past + ln) * h)
    new = pl.BlockSpec((1, ln * h, hd), lambda i, c: (i, 0, 0))
    cache = pl.BlockSpec((1, ch, hd), lambda i, c: (i + cache_off, c, 0))
    o = pl.pallas_call(
        functools.partial(_fox_sample_body, h),
        grid=(b, past // chunk_steps),
        in_specs=[new, cache, cache, new, new,
                  pl.BlockSpec((1, h * ln, 1), lambda i, c: (i, 0, 0)),
                  pl.BlockSpec((1, 1, ch), lambda i, c: (i, 0, c)),
                  pl.BlockSpec((1, 1, ln * h), lambda i, c: (i, 0, 0))],
        out_specs=new,
        out_shape=jax.ShapeDtypeStruct((b, h * ln, hd), BF16),
        scratch_shapes=[pltpu.VMEM((h * ln, 1), F32), pltpu.VMEM((h * ln, 1), F32), pltpu.VMEM((h * ln, hd), F32)],
        compiler_params=_params(("parallel", "arbitrary")),
        name="fox_sample",
    )(q_rows, cache_k.reshape(rows_total, past * h, hd), cache_v.reshape(rows_total, past * h, hd),
      fk.reshape(b, ln * h, hd), fv.reshape(b, ln * h, hd), f_q, f_k[:, :, :past * h], f_k[:, :, past * h:])
    return o.reshape(b, h, ln, hd).transpose(0, 2, 1, 3).reshape(b, ln, d)


def _mem_kv_body(mem_ref, g_ref, w_ref, k_ref, v_ref):
    u = _rmsnorm(mem_ref[0], g_ref[...]).astype(BF16)
    _, _, heads, hd = k_ref.shape
    d = heads * hd
    k = _dot(u, w_ref[:, :d])
    v = _dot(u, w_ref[:, d:])
    for hh in range(heads):
        k_ref[0, :, hh, :] = k[:, hh * hd:(hh + 1) * hd]
        v_ref[0, :, hh, :] = v[:, hh * hd:(hh + 1) * hd]


def _mem_kv(mem, g, w):
    b, n, d = mem.shape
    hd = d // MEM_HEADS
    out = pl.BlockSpec((1, n, MEM_HEADS, hd), lambda i: (i, 0, 0, 0))
    return pl.pallas_call(
        _mem_kv_body,
        grid=(b,),
        in_specs=[pl.BlockSpec((1, n, d), lambda i: (i, 0, 0)), _resident(g.shape), _resident(w.shape)],
        out_specs=[out, out],
        out_shape=[jax.ShapeDtypeStruct((b, n, MEM_HEADS, hd), F32)] * 2,
        compiler_params=_params(("parallel",)),
        name="mem_kv",
    )(mem, g, w)


def _mix_out_body(seq, x_ref, og_ref, of_ref, mk_ref, mv_ref, pre_ref, w2_ref, gn_ref, wgo_ref, wfo_ref, wmo_ref,
                  wout_ref, post_ref, o_ref, u_ref):
    x = x_ref[...]
    tm, d = x.shape
    u_ref[...] = _rmsnorm(x, pre_ref[...]).astype(BF16)

    def proj(j):
        return _dot(u_ref[...], w2_ref[:, j * d:(j + 1) * d])

    mq = proj(1)
    hd = d // MEM_HEADS
    scale = float(hd ** -0.5)
    per_batch = []
    for j in range(tm // seq):
        heads = []
        for hh in range(MEM_HEADS):
            sl = slice(hh * hd, (hh + 1) * hd)
            s = _dot_nt(mq[j * seq:(j + 1) * seq, sl].astype(BF16), mk_ref[j, :, hh, :].astype(BF16)) * scale
            e = jnp.exp(s - jnp.max(s, axis=1, keepdims=True))
            p = e / jnp.sum(e, axis=1, keepdims=True)
            heads.append(_dot(p.astype(BF16), mv_ref[j, :, hh, :].astype(BF16)))
        per_batch.append(jnp.concatenate(heads, axis=1))
    o_mem = per_batch[0] if len(per_batch) == 1 else jnp.concatenate(per_batch, axis=0)
    y = _sigmoid(proj(4)) * _dot(o_mem.astype(BF16), wmo_ref[...])

    y = y + _sigmoid(proj(3)) * _dot(of_ref[...], wfo_ref[...])

    og = og_ref[...].astype(F32)
    gd = d // GLA_HEADS
    gn = gn_ref[...]
    normed = []
    for hh in range(GLA_HEADS):
        sl = slice(hh * gd, (hh + 1) * gd)
        normed.append(_rmsnorm(og[:, sl], gn[:, sl]))
    r = proj(0)
    b_gla = _dot((jnp.concatenate(normed, axis=1) * (r * _sigmoid(r))).astype(BF16), wgo_ref[...])
    y = y + _sigmoid(proj(2)) * b_gla

    mixed = _dot(y.astype(BF16), wout_ref[...])
    o_ref[...] = x + _rmsnorm(mixed, post_ref[...])


def _mix_out(x2, o_gla, o_fox, mk, mv, mem_off, seq, pre_g, w2, gn, wgo, wfo, wmo, wout, post_g, tm=512):
    m, d = x2.shape
    if seq < tm:
        nb = 4
        while (m // seq) % nb:
            nb //= 2
        tm = seq * nb
    assert m % tm == 0 and (seq % tm == 0 or tm % seq == 0)
    nb = max(1, tm // seq)
    steps_per_batch = max(1, seq // tm)
    mem = mk.shape[1]
    rows = pl.BlockSpec((tm, d), lambda i: (i, 0))
    assert mem_off % nb == 0
    memb = pl.BlockSpec((nb, mem) + mk.shape[2:], lambda i: (i // steps_per_batch + mem_off // nb, 0, 0, 0))
    return pl.pallas_call(
        functools.partial(_mix_out_body, min(seq, tm)),
        grid=(m // tm,),
        in_specs=[rows, rows, rows, memb, memb, _resident(pre_g.shape), _resident(w2.shape), _resident(gn.shape),
                  _resident(wgo.shape), _resident(wfo.shape), _resident(wmo.shape), _resident(wout.shape),
                  _resident(post_g.shape)],
        out_specs=rows,
        out_shape=jax.ShapeDtypeStruct((m, d), F32),
        scratch_shapes=[pltpu.VMEM((tm, d), BF16)],
        compiler_params=_params(("parallel",)),
        name="mix_out",
    )(x2, o_gla, o_fox, mk, mv, pre_g, w2, gn, wgo, wfo, wmo, wout, post_g)


def _prep_layer(d, w):
    gk_w = d // 2
    sizes = (gk_w, gk_w, d, GLA_RANK, d, d, d, d, FOX_HEADS, d, N_BRANCH * d)
    offs = np.concatenate([[0], np.cumsum(sizes)]).tolist()
    seg = {n: (offs[i], offs[i + 1]) for i, n in enumerate(
        ("gq", "gk", "gv", "glr", "gr", "fq", "fk", "fv", "ff", "mq", "gates"))}
    w_in = w["w_in"]

    def cols(*names):
        return jnp.concatenate([w_in[:, seg[n][0]:seg[n][1]] for n in names], axis=1).astype(BF16)

    ws = jnp.pad(cols("ff", "glr"), ((0, 0), (0, SMALL_W - FOX_HEADS - GLA_RANK)))
    wa2p = jnp.pad(w["gla_w_a2"].astype(BF16), ((FOX_HEADS, SMALL_W - FOX_HEADS - GLA_RANK), (0, 0)))
    b_f = jnp.pad(w["fox_b_f"], (0, SMALL_W - FOX_HEADS)).reshape(1, SMALL_W)

    def ffn(pfx):
        wg, wu, wd = w[pfx + "_w_gate"], w[pfx + "_w_up"], w[pfx + "_w_down"]
        assert wg.shape[1] % FFN_CHUNK == 0
        return dict(pre=w[pfx + "_pre_g"].reshape(1, d), post=w[pfx + "_post_g"].reshape(1, d),
                    wg=wg.astype(BF16), wu=wu.astype(BF16), wd=wd.astype(BF16))

    return dict(
        ffn1=ffn("ffn1"), ffn2=ffn("ffn2"),
        mix_pre=w["mix_pre_g"].reshape(1, d), mix_post=w["mix_post_g"].reshape(1, d),
        w1=cols("gq", "gk", "gv", "fq", "fk", "fv"), ws=ws, wa2p=wa2p,
        b_a=w["gla_b_a"].reshape(1, gk_w), b_f=b_f,
        w2=cols("gr", "mq", "gates"),
        gn=w["gla_norm_g"].reshape(1, d),
        wgo=w["w_gla_o"].astype(BF16), wfo=w["w_fox_o"].astype(BF16), wmo=w["w_mem_o"].astype(BF16),
        wout=w["w_out"].astype(BF16),
        mem_g=w["mem_norm_g"].reshape(1, d), w_mem_kv=w["w_mem_kv"].astype(BF16),
    )


def _ffn_apply(x2, p):
    return _ffn(x2, p["pre"], p["post"], p["wg"], p["wu"], p["wd"])


def _forget_layouts(f, lead=None):
    b, t, h = f.shape
    if lead is not None:
        f = jnp.concatenate([lead.astype(F32), f], axis=1)
    total = f.shape[1]
    padded = -(-total // LANES) * LANES
    f = jnp.pad(f, ((0, 0), (0, padded - total), (0, 0)))
    rows = _cumsum_time(jnp.swapaxes(f, 1, 2).reshape(b * h, padded // LANES, LANES))
    rows = rows.reshape(b, h, padded)
    cols = jnp.swapaxes(rows[:, :, total - t:total], 1, 2)
    return cols, rows


def _group_layer(x2, b, t, p, layer, depth, carried, gla_state, fox, memory):
    d = x2.shape[1]
    x2 = _ffn_apply(x2, p["ffn1"])
    gq, gk, gv, la, fq, fk16, fv16, *carried = _mix_in(x2, p["mix_pre"], p["w1"], p["ws"], p["wa2p"], p["b_a"],
                                                       p["b_f"], layer, depth, carried)
    r3 = lambda a: a.reshape(b, t, a.shape[1])
    o_gla, s_fin = _gla(r3(gq), r3(gk), r3(gv), r3(la), gla_state[0], gla_state[1])
    f_log = carried[2][layer].reshape(b, t, SMALL_W)[:, :, :FOX_HEADS]
    o_fox = fox(r3(fq), r3(fk16), r3(fv16), f_log)
    x2 = _mix_out(x2, o_gla.reshape(b * t, d), o_fox.reshape(b * t, d), memory[0], memory[1], memory[2], t,
                  p["mix_pre"], p["w2"], p["gn"], p["wgo"], p["wfo"], p["wmo"], p["wout"], p["mix_post"])
    x2 = _ffn_apply(x2, p["ffn2"])
    return x2, carried, s_fin


def kernel(x_prompt, x_sample, cache_fox_k, cache_fox_v, cache_fox_logf, state_gla, cache_mem_k, cache_mem_v, mem_prompt, ffn1_pre_g, ffn1_post_g, ffn1_w_gate, ffn1_w_up, ffn1_w_down, mix_pre_g, mix_post_g, w_in, gla_w_a2, gla_b_a, fox_b_f, gla_norm_g, w_gla_o, w_fox_o, w_mem_o, w_out, mem_norm_g, w_mem_kv, ffn2_pre_g, ffn2_post_g, ffn2_w_gate, ffn2_w_up, ffn2_w_down):
    weights = dict(ffn1_pre_g=ffn1_pre_g, ffn1_post_g=ffn1_post_g, ffn1_w_gate=ffn1_w_gate, ffn1_w_up=ffn1_w_up,
                   ffn1_w_down=ffn1_w_down, mix_pre_g=mix_pre_g, mix_post_g=mix_post_g, w_in=w_in,
                   gla_w_a2=gla_w_a2, gla_b_a=gla_b_a, fox_b_f=fox_b_f, gla_norm_g=gla_norm_g, w_gla_o=w_gla_o,
                   w_fox_o=w_fox_o, w_mem_o=w_mem_o, w_out=w_out, mem_norm_g=mem_norm_g, w_mem_kv=w_mem_kv,
                   ffn2_pre_g=ffn2_pre_g, ffn2_post_g=ffn2_post_g, ffn2_w_gate=ffn2_w_gate, ffn2_w_up=ffn2_w_up,
                   ffn2_w_down=ffn2_w_down)
    depth = w_in.shape[0]
    bp, tp, d = x_prompt.shape
    bs, ts, _ = x_sample.shape
    hd = d // FOX_HEADS
    mh = d // MEM_HEADS
    dk, dv = state_gla.shape[-2:]
    layers = [_prep_layer(d, {k: v[l] for k, v in weights.items()}) for l in range(depth)]

    x = x_prompt.reshape(bp * tp, d)
    zero_state = jnp.zeros((bp, GLA_HEADS, dk, dv), F32)
    tq = min(512, tp)

    def fox_p(fq, fk, fv, f_log):
        f_cols, _ = _forget_layouts(f_log)
        return _fox_prompt(_fox_augment(fq, f_cols, True), _fox_augment(fk, f_cols, False), fv, tq=tq)

    carried, p_state, p_mk, p_mv = [], [], [], []
    for l in range(depth):
        mk, mv = _mem_kv(mem_prompt, layers[l]["mem_g"], layers[l]["w_mem_kv"])
        x, carried, s_fin = _group_layer(x, bp, tp, layers[l], l, depth, carried, (zero_state, 0), fox_p, (mk, mv, 0))
        p_state.append(s_fin)
        p_mk.append(mk)
        p_mv.append(mv)
    y_prompt = x.reshape(bp, tp, d)
    p_fox_k = carried[0].reshape(depth, bp, tp, FOX_HEADS, hd)
    p_fox_v = carried[1].reshape(depth, bp, tp, FOX_HEADS, hd)
    p_fox_logf = carried[2].reshape(depth, bp, tp, SMALL_W)[..., :FOX_HEADS]
    mem_len = mem_prompt.shape[1]
    p_mem_k = jnp.stack(p_mk)
    p_mem_v = jnp.stack(p_mv)

    x = x_sample.reshape(bs * ts, d)
    past = cache_fox_k.shape[2]
    ck = cache_fox_k.reshape(depth * bs, past, FOX_HEADS, hd)
    cv = cache_fox_v.reshape(depth * bs, past, FOX_HEADS, hd)
    states = state_gla.reshape(depth * bs, GLA_HEADS, dk, dv)
    mem_s = cache_mem_k.shape[2]
    cmk = cache_mem_k.reshape(depth * bs, mem_s, MEM_HEADS, mh)
    cmv = cache_mem_v.reshape(depth * bs, mem_s, MEM_HEADS, mh)
    carried, s_state = [], []
    for l in range(depth):
        def fox_s(fq, fk, fv, f_log, l=l):
            _, f_rows = _forget_layouts(f_log, lead=cache_fox_logf[l])
            f_all = jnp.swapaxes(f_rows[:, :, :past + ts], 1, 2)
            return _fox_sample(fq, fk, fv, ck, cv, l * bs, f_all)

        x, carried, s_new = _group_layer(x, bs, ts, layers[l], l, depth, carried, (states, l * bs), fox_s,
                                         (cmk, cmv, l * bs))
        s_state.append(s_new)
    y_sample = x.reshape(bs, ts, d)
    s_fox_k = carried[0].reshape(depth, bs, ts, FOX_HEADS, hd)
    s_fox_v = carried[1].reshape(depth, bs, ts, FOX_HEADS, hd)
    s_fox_logf = carried[2].reshape(depth, bs, ts, SMALL_W)[..., :FOX_HEADS]

    return (y_prompt, y_sample, p_fox_k, p_fox_v, p_fox_logf, jnp.stack(p_state), p_mem_k, p_mem_v,
            s_fox_k, s_fox_v, s_fox_logf, jnp.stack(s_state))
```

```python
import functools

import numpy as np
import jax
import jax.numpy as jnp
from jax import lax
from jax.experimental import pallas as pl
from jax.experimental.pallas import tpu as pltpu

F32 = jnp.float32
BF16 = jnp.bfloat16

EPS = 1e-6
GLA_HEADS = 4
GLA_RANK = 16
GLA_TAU = 16.0
FOX_HEADS = 8
MEM_HEADS = 4
N_BRANCH = 3

LANES = 128
SUBLANES = 8
V7X_VMEM_LIMIT = 56 * 1024 * 1024
FFN_CHUNK = 256
GLA_CHUNK = 64
GLA_SUB = 16
FOX_HEADS_PER_STEP = 4
FOX_WIDE = 2
LOG2E = 1.4426950408889634
SMALL_W = 128


def _dot(a, b):
    return jnp.dot(a, b, preferred_element_type=F32)


def _dot_nt(a, b):
    return lax.dot_general(a, b, (((1,), (1,)), ((), ())), preferred_element_type=F32)


def _dot_tn(a, b):
    return lax.dot_general(a, b, (((0,), (0,)), ((), ())), preferred_element_type=F32)


def _rmsnorm(x, g):
    return x * lax.rsqrt(jnp.mean(x * x, axis=-1, keepdims=True) + EPS) * g


def _sigmoid(x):
    return 1.0 / (1.0 + jnp.exp(-x))


def _log_sigmoid(x):
    return jnp.minimum(x, 0.0) - jnp.log1p(jnp.exp(-jnp.abs(x)))


def _split3(x):
    hi = x.astype(BF16)
    r1 = x - hi.astype(F32)
    mid = r1.astype(BF16)
    lo = (r1 - mid.astype(F32)).astype(BF16)
    return hi, mid, lo


def _resident(shape):
    nd = len(shape)
    return pl.BlockSpec(shape, lambda *_: (0,) * nd, pipeline_mode=pl.Buffered(1))


def _params(semantics):
    return pltpu.CompilerParams(dimension_semantics=semantics, vmem_limit_bytes=V7X_VMEM_LIMIT)


def _row_tile(m, want):
    t = min(m, want)
    assert m % t == 0, (m, t)
    return t


def _ffn_body(x_ref, pre_ref, post_ref, wg_ref, wu_ref, wd_ref, o_ref, u_ref, acc_ref):
    x = x_ref[...]
    u_ref[...] = _rmsnorm(x, pre_ref[...]).astype(BF16)
    for c in range(wg_ref.shape[1] // FFN_CHUNK):
        cols = slice(c * FFN_CHUNK, (c + 1) * FFN_CHUNK)
        u = u_ref[...]
        g = _dot(u, wg_ref[:, cols])
        up = _dot(u, wu_ref[:, cols])
        h = (g * _sigmoid(g) * up).astype(BF16)
        part = _dot(h, wd_ref[cols, :])
        if c == 0:
            acc_ref[...] = part
        else:
            acc_ref[...] += part
    o_ref[...] = x + 0.5 * _rmsnorm(acc_ref[...], post_ref[...])


def _ffn(x2, pre_g, post_g, wg, wu, wd, tm=512):
    m, d = x2.shape
    tm = _row_tile(m, tm)
    return pl.pallas_call(
        _ffn_body,
        grid=(m // tm,),
        in_specs=[pl.BlockSpec((tm, d), lambda i: (i, 0)),
                  _resident(pre_g.shape), _resident(post_g.shape),
                  _resident(wg.shape), _resident(wu.shape), _resident(wd.shape)],
        out_specs=pl.BlockSpec((tm, d), lambda i: (i, 0)),
        out_shape=jax.ShapeDtypeStruct((m, d), F32),
        scratch_shapes=[pltpu.VMEM((tm, d), BF16), pltpu.VMEM((tm, d), F32)],
        compiler_params=_params(("parallel",)),
        name="ffn",
    )(x2, pre_g, post_g, wg, wu, wd)


def _mix_in_body(dims, n_carried, x_ref, pre_ref, w_ref, ws_ref, wa2_ref, ba_ref, bf_ref, *refs):
    gq_ref, gk_ref, gv_ref, la_ref, fq_ref, fk16_ref, fv16_ref, fk32_ref, fv32_ref, fl_ref, u_ref = refs[n_carried:]
    gk_w, gv_w, fx_w, dk_scale, hd_scale = dims
    u_ref[...] = _rmsnorm(x_ref[...], pre_ref[...]).astype(BF16)

    def proj(lo, width):
        return _dot(u_ref[...], w_ref[:, lo:lo + width])

    off = 0
    gq_ref[...] = proj(off, gk_w) * dk_scale
    off += gk_w
    gk_ref[...] = proj(off, gk_w)
    off += gk_w
    gv_ref[...] = proj(off, gv_w).astype(BF16)
    off += gv_w
    fq_ref[...] = (proj(off, fx_w) * hd_scale).astype(BF16)
    off += fx_w
    hd = fx_w // FOX_HEADS
    fk = proj(off, fx_w)
    fk16_ref[...] = fk.astype(BF16)
    off += fx_w
    fv = proj(off, fx_w)
    fv16_ref[...] = fv.astype(BF16)
    for hh in range(FOX_HEADS):
        fk32_ref[:, hh, :] = fk[:, hh * hd:(hh + 1) * hd]
        fv32_ref[:, hh, :] = fv[:, hh * hd:(hh + 1) * hd]

    small = _dot(u_ref[...], ws_ref[...])
    lane = lax.broadcasted_iota(jnp.int32, small.shape, 1)
    fl_ref[...] = jnp.where(lane < FOX_HEADS, _log_sigmoid(small + bf_ref[...]), 0.0)
    lr = jnp.where((lane >= FOX_HEADS) & (lane < FOX_HEADS + GLA_RANK), small, 0.0).astype(BF16)
    la_ref[...] = _log_sigmoid(_dot(lr, wa2_ref[...]) + ba_ref[...]) * (1.0 / GLA_TAU)


def _mix_in(x2, pre_g, w1, ws, wa2p, b_a, b_f, layer, depth, carried, tm=512):
    m, d = x2.shape
    tm = _row_tile(m, tm)
    gk_w = d // 2
    dims = (gk_w, d, d, float((gk_w // GLA_HEADS) ** -0.5), float((d // FOX_HEADS) ** -0.5) * LOG2E)

    def rows(width):
        return pl.BlockSpec((tm, width), lambda i: (i, 0))

    def stacked(*tail):
        return pl.BlockSpec((None, tm) + tail, lambda i: (layer, i) + (0,) * len(tail))

    outs = [(gk_w, F32), (gk_w, F32), (d, BF16), (gk_w, F32), (d, BF16), (d, BF16), (d, BF16)]
    stacks = [(FOX_HEADS, d // FOX_HEADS), (FOX_HEADS, d // FOX_HEADS), (SMALL_W,)]
    n_in = 7
    return pl.pallas_call(
        functools.partial(_mix_in_body, dims, len(carried)),
        grid=(m // tm,),
        in_specs=[rows(d), _resident(pre_g.shape), _resident(w1.shape), _resident(ws.shape),
                  _resident(wa2p.shape), _resident(b_a.shape), _resident(b_f.shape)]
                 + [pl.BlockSpec(memory_space=pl.ANY)] * len(carried),
        out_specs=[rows(w) for w, _ in outs] + [stacked(*tail) for tail in stacks],
        out_shape=[jax.ShapeDtypeStruct((m, w), dt) for w, dt in outs]
                  + [jax.ShapeDtypeStruct((depth, m) + tail, F32) for tail in stacks],
        input_output_aliases={n_in + j: len(outs) + j for j in range(len(carried))},
        scratch_shapes=[pltpu.VMEM((tm, d), BF16)],
        compiler_params=_params(("parallel",)),
        name="mix_in",
    )(x2, pre_g, w1, ws, wa2p, b_a, b_f, *carried)


def _cumsum_body(x_ref, m_ref, o_ref):
    x = x_ref[...]
    r = lax.broadcasted_iota(jnp.int32, (LANES, LANES), 0)
    c = lax.broadcasted_iota(jnp.int32, (LANES, LANES), 1)
    upper = jnp.where(r <= c, 1.0, 0.0).astype(BF16)
    hi, mid, lo = _split3(x)
    o_ref[...] = _dot(hi, upper) + _dot(mid, upper) + _dot(lo, upper)
    tot = jnp.broadcast_to(o_ref[:, LANES - 1:LANES], x.shape)
    hi, mid, lo = _split3(tot)
    mm = m_ref[...]
    offset = _dot(mm, hi) + _dot(mm, mid) + _dot(mm, lo)
    o_ref[...] = o_ref[...] + offset


def _cumsum_time(x3):
    g, n, _ = x3.shape
    per = 8
    while (per * n) % 8:
        per += 1
    gpb = per if g % per == 0 else g
    rb = gpb * n
    idx = np.arange(rb)
    earlier = (idx[:, None] // n == idx[None, :] // n) & (idx[None, :] < idx[:, None])
    mm = jnp.asarray(earlier.astype(np.float32), BF16)
    out = pl.pallas_call(
        _cumsum_body,
        grid=(g // gpb,),
        in_specs=[pl.BlockSpec((rb, LANES), lambda i: (i, 0)), _resident((rb, rb))],
        out_specs=pl.BlockSpec((rb, LANES), lambda i: (i, 0)),
        out_shape=jax.ShapeDtypeStruct((g * n, LANES), F32),
        compiler_params=_params(("parallel",)),
        name="cumsum",
    )(x3.reshape(g * n, LANES), mm)
    return out.reshape(g, n, LANES)


def _gla_chunk(q, k, v, la, st):
    cl, dk = q.shape
    sub = min(GLA_SUB, cl)
    row = lax.broadcasted_iota(jnp.int32, (cl, cl), 0)
    col = lax.broadcasted_iota(jnp.int32, (cl, cl), 1)
    lower = jnp.where(row >= col, 1.0, 0.0).astype(BF16)
    hi, mid, lo = _split3(la)
    b = _dot(lower, hi) + _dot(lower, mid) + _dot(lower, lo)
    b_end = b[cl - 1:cl, :]

    o = _dot_nt((q * jnp.exp(b)).astype(BF16), st.astype(BF16))

    tsub = lax.broadcasted_iota(jnp.int32, (SUBLANES, dk), 0)
    acol = lax.broadcasted_iota(jnp.int32, (SUBLANES, cl), 1)
    blocks = []
    for i in range(cl // sub):
        r0 = i * sub
        b_i, q_i, k_i = b[r0:r0 + sub], q[r0:r0 + sub], k[r0:r0 + sub]
        groups = [jnp.zeros((SUBLANES, cl), F32) for _ in range(sub // SUBLANES)]
        for s in range(sub):
            own = s // SUBLANES
            for gi in range(own, sub // SUBLANES):
                rows = slice(gi * SUBLANES, (gi + 1) * SUBLANES)
                rel = b_i[rows] - b_i[s:s + 1]
                if gi == own:
                    rel = jnp.where(tsub >= s - gi * SUBLANES, rel, -jnp.inf)
                w = jnp.sum(q_i[rows] * k_i[s:s + 1] * jnp.exp(rel), axis=1, keepdims=True)
                groups[gi] = jnp.where(acol == r0 + s, w, groups[gi])
        a_i = jnp.concatenate(groups, axis=0)
        if i > 0:
            ref = b[r0 - 1:r0]
            q_t = (q_i * jnp.exp(b_i - ref)).astype(BF16)
            k_t = jnp.concatenate([k[:r0] * jnp.exp(ref - b[:r0]), jnp.zeros((cl - r0, dk), F32)], axis=0)
            a_i = a_i + _dot_nt(q_t, k_t.astype(BF16))
        blocks.append(a_i)
    attn = jnp.concatenate(blocks, axis=0).astype(BF16)
    o = o + _dot(attn, v)

    k_d = (k * jnp.exp(b_end - b)).astype(BF16)
    st_new = st * jnp.exp(b_end) + _dot_tn(v, k_d)
    return o, st_new


def _gla_body(cl, q_ref, k_ref, v_ref, la_ref, s0_ref, o_ref, sf_ref, st_ref):
    t = pl.program_id(1)
    heads, dv, dk = st_ref.shape

    @pl.when(t == 0)
    def _():
        for hh in range(heads):
            st_ref[hh] = s0_ref[0, hh].T

    n_chunks = q_ref.shape[1] // cl

    def chunk(ci, carry):
        r = pl.multiple_of(ci * cl, cl)
        for hh in range(heads):
            ks = slice(hh * dk, (hh + 1) * dk)
            vs = slice(hh * dv, (hh + 1) * dv)
            o, st = _gla_chunk(q_ref[0, pl.ds(r, cl), ks], k_ref[0, pl.ds(r, cl), ks], v_ref[0, pl.ds(r, cl), vs],
                               la_ref[0, pl.ds(r, cl), ks], st_ref[hh])
            o_ref[0, pl.ds(r, cl), vs] = o.astype(o_ref.dtype)
            st_ref[hh] = st
        return carry

    lax.fori_loop(0, n_chunks, chunk, 0, unroll=2 if n_chunks % 2 == 0 else 1)

    @pl.when(t == pl.num_programs(1) - 1)
    def _():
        for hh in range(heads):
            sf_ref[0, hh] = st_ref[hh].T


def _gla(gq, gk, gv, la, s0, s0_off=0, tt=512):
    b, t, _ = gq.shape
    _, h, dk, dv = s0.shape
    cl = min(GLA_CHUNK, t)
    tt = _row_tile(t, tt)
    return pl.pallas_call(
        functools.partial(_gla_body, cl),
        grid=(b, t // tt),
        in_specs=[pl.BlockSpec((1, tt, h * dk), lambda i, n: (i, n, 0)),
                  pl.BlockSpec((1, tt, h * dk), lambda i, n: (i, n, 0)),
                  pl.BlockSpec((1, tt, h * dv), lambda i, n: (i, n, 0)),
                  pl.BlockSpec((1, tt, h * dk), lambda i, n: (i, n, 0)),
                  pl.BlockSpec((1, h, dk, dv), lambda i, n: (i + s0_off, 0, 0, 0))],
        out_specs=[pl.BlockSpec((1, tt, h * dv), lambda i, n: (i, n, 0)),
                   pl.BlockSpec((1, h, dk, dv), lambda i, n: (i, 0, 0, 0))],
        out_shape=[jax.ShapeDtypeStruct((b, t, h * dv), BF16), jax.ShapeDtypeStruct((b, h, dk, dv), F32)],
        scratch_shapes=[pltpu.VMEM((h, dv, dk), F32)],
        compiler_params=_params(("parallel", "arbitrary")),
        name="gla",
    )(gq, gk, gv, la, s0)


def _fox_prompt_body(hps, q_ref, k_ref, v_ref, o_ref, m_ref, acc_ref):
    qi = pl.program_id(2)
    tq = q_ref.shape[1]
    hw = q_ref.shape[2] // hps
    hd = v_ref.shape[2] // hps
    wide = FOX_WIDE * tq

    def logits(hh, r, width):
        sl = slice(hh * hw, (hh + 1) * hw)
        return _dot_nt(q_ref[0, :, sl], k_ref[0, pl.ds(r, width), sl])

    def values(hh, r, width):
        lane = lax.broadcasted_iota(jnp.int32, (width, LANES), 1)
        ones_col = jnp.where(lane == 0, 1.0, 0.0).astype(BF16)
        return jnp.concatenate([v_ref[0, pl.ds(r, width), hh * hd:(hh + 1) * hd], ones_col], axis=1)

    def first(r, width):
        row = lax.broadcasted_iota(jnp.int32, (tq, width), 0)
        col = lax.broadcasted_iota(jnp.int32, (tq, width), 1)
        for hh in range(hps):
            z = jnp.where(col <= row + (width - tq), logits(hh, r, width), -jnp.inf)
            m = jnp.max(z, axis=1, keepdims=True)
            m_ref[hh] = m
            acc_ref[hh] = _dot(jnp.exp2(z - m).astype(BF16), values(hh, r, width))

    assert FOX_WIDE == 2
    lead = (qi + 1) % FOX_WIDE

    @pl.when(lead == 1)
    def _():
        first(pl.multiple_of(qi * tq, tq), tq)

    @pl.when(lead == 0)
    def _():
        first(pl.multiple_of((qi + 1) * tq - wide, tq), wide)

    def update(r, width):
        for hh in range(hps):
            z = logits(hh, r, width)
            m_old = m_ref[hh]
            m_new = jnp.maximum(m_old, jnp.max(z, axis=1, keepdims=True))
            m_ref[hh] = m_new
            pv = _dot(jnp.exp2(z - m_new).astype(BF16), values(hh, r, width))
            acc_ref[hh] = jnp.exp2(m_old - m_new) * acc_ref[hh] + pv

    n_wide = qi // FOX_WIDE

    @pl.when(n_wide % 2 == 1)
    def _():
        update(0, wide)

    base = (n_wide % 2) * wide

    def double_wide_block(j, carry):
        update(pl.multiple_of(base + j * 2 * wide, wide), 2 * wide)
        return carry

    lax.fori_loop(0, n_wide // 2, double_wide_block, 0)
    for hh in range(hps):
        acc = acc_ref[hh]
        o_ref[0, :, hh * hd:(hh + 1) * hd] = (acc[:, :hd] / acc[:, hd:hd + 1]).astype(o_ref.dtype)


def _fox_augment(x, f_cols, query):
    b, t, d = x.shape
    h = f_cols.shape[2]
    hd = d // h

    def top16(v):
        return lax.bitcast_convert_type(lax.bitcast_convert_type(v, jnp.uint32) & jnp.uint32(0xFFFF0000), F32)

    f = f_cols * LOG2E
    hi = top16(f)
    mid = top16(f - hi)
    lo = f - hi - mid
    one = jnp.ones_like(f)
    cols = (hi, mid, lo, one, one, one) if query else (one, one, one, -hi, -mid, -lo)
    lane = lax.broadcasted_iota(jnp.int32, (1, 1, hd), 2)
    parts = []
    for hh in range(h):
        extra = jnp.zeros((b, t, hd), F32)
        for j, c in enumerate(cols):
            extra = jnp.where(lane == j, c[:, :, hh:hh + 1], extra)
        parts += [x[:, :, hh * hd:(hh + 1) * hd], extra.astype(BF16)]
    return jnp.concatenate(parts, axis=-1)


def _fox_prompt(q_aug, k_aug, fv, tq=512, hps=FOX_HEADS_PER_STEP):
    b, t, d = fv.shape
    h = FOX_HEADS
    hd = d // h
    nq = t // tq
    keys = pl.BlockSpec((1, t, hps * 2 * hd), lambda i, j, n: (i, 0, j), pipeline_mode=pl.Buffered(1))
    vals = pl.BlockSpec((1, t, hps * hd), lambda i, j, n: (i, 0, j), pipeline_mode=pl.Buffered(1))
    return pl.pallas_call(
        functools.partial(_fox_prompt_body, hps),
        grid=(b, h // hps, nq),
        in_specs=[pl.BlockSpec((1, tq, hps * 2 * hd), lambda i, j, n: (i, n, j)), keys, vals],
        out_specs=pl.BlockSpec((1, tq, hps * hd), lambda i, j, n: (i, n, j)),
        out_shape=jax.ShapeDtypeStruct((b, t, d), BF16),
        scratch_shapes=[pltpu.VMEM((hps, tq, 1), F32), pltpu.VMEM((hps, tq, hd + LANES), F32)],
        compiler_params=_params(("parallel", "parallel", "arbitrary")),
        name="fox_prompt",
    )(q_aug, k_aug, fv)


def _fox_sample_body(heads, q_ref, kc_ref, vc_ref, kn_ref, vn_ref, fq_ref, fkc_ref, fkn_ref, o_ref,
                     m_ref, l_ref, acc_ref):
    c = pl.program_id(1)
    q = q_ref[0]
    rows = q.shape[0]
    ln = rows // heads
    f_q = fq_ref[0] * LOG2E
    head_q = lax.broadcasted_iota(jnp.int32, (rows, 1), 0) // ln

    def head_k(n):
        return lax.broadcasted_iota(jnp.int32, (1, n), 1) % heads

    @pl.when(c == 0)
    def _():
        n = kn_ref.shape[1]
        z = _dot_nt(q, kn_ref[0]) + f_q - fkn_ref[0] * LOG2E
        step_k = lax.broadcasted_iota(jnp.int32, (1, n), 1) // heads
        step_q = lax.broadcasted_iota(jnp.int32, (rows, 1), 0) % ln
        z = jnp.where((head_k(n) == head_q) & (step_k <= step_q), z, -jnp.inf)
        m = jnp.max(z, axis=1, keepdims=True)
        p = jnp.exp2(z - m)
        m_ref[...] = m
        l_ref[...] = jnp.sum(p, axis=1, keepdims=True)
        acc_ref[...] = _dot(p.astype(BF16), vn_ref[0])

    n = kc_ref.shape[1]
    z = _dot_nt(q, kc_ref[0].astype(BF16)) + f_q - fkc_ref[0] * LOG2E
    z = jnp.where(head_k(n) == head_q, z, -jnp.inf)
    m_old = m_ref[...]
    m_new = jnp.maximum(m_old, jnp.max(z, axis=1, keepdims=True))
    alpha = jnp.exp2(m_old - m_new)
    p = jnp.exp2(z - m_new)
    m_ref[...] = m_new
    l_ref[...] = alpha * l_ref[...] + jnp.sum(p, axis=1, keepdims=True)
    acc_ref[...] = alpha * acc_ref[...] + _dot(p.astype(BF16), vc_ref[0].astype(BF16))

    @pl.when(c == pl.num_programs(1) - 1)
    def _():
        o_ref[0] = (acc_ref[...] / l_ref[...]).astype(o_ref.dtype)


def _fox_sample(fq, fk, fv, cache_k, cache_v, cache_off, f_all, chunk_steps=512):
    b, ln, d = fq.shape
    rows_total, past, h, hd = cache_k.shape
    assert h & (h - 1) == 0 and ln & (ln - 1) == 0
    chunk_steps = _row_tile(past, chunk_steps)
    ch = chunk_steps * h
    q_rows = fq.reshape(b, ln, h, hd).transpose(0, 2, 1, 3).reshape(b, h * ln, hd)
    f_q = jnp.swapaxes(f_all[:, past:, :], 1, 2).reshape(b, h * ln, 1)
    f_k = f_all.reshape(b, 1, (past + ln) * h)
    new = pl.BlockSpec((1, ln * h, hd), lambda i, c: (i, 0, 0))
    cache = pl.BlockSpec((1, ch, hd), lambda i, c: (i + cache_off, c, 0))
    o = pl.pallas_call(
        functools.partial(_fox_sample_body, h),
        grid=(b, past // chunk_steps),
        in_specs=[new, cache, cache, new, new,
                  pl.BlockSpec((1, h * ln, 1), lambda i, c: (i, 0, 0)),
                  pl.BlockSpec((1, 1, ch), lambda i, c: (i, 0, c)),
                  pl.BlockSpec((1, 1, ln * h), lambda i, c: (i, 0, 0))],
        out_specs=new,
        out_shape=jax.ShapeDtypeStruct((b, h * ln, hd), BF16),
        scratch_shapes=[pltpu.VMEM((h * ln, 1), F32), pltpu.VMEM((h * ln, 1), F32), pltpu.VMEM((h * ln, hd), F32)],
        compiler_params=_params(("parallel", "arbitrary")),
        name="fox_sample",
    )(q_rows, cache_k.reshape(rows_total, past * h, hd), cache_v.reshape(rows_total, past * h, hd),
      fk.reshape(b, ln * h, hd), fv.reshape(b, ln * h, hd), f_q, f_k[:, :, :past * h], f_k[:, :, past * h:])
    return o.reshape(b, h, ln, hd).transpose(0, 2, 1, 3).reshape(b, ln, d)


def _mem_kv_body(mem_ref, g_ref, w_ref, k_ref, v_ref, k16_ref, v16_ref):
    u = _rmsnorm(mem_ref[0], g_ref[...]).astype(BF16)
    _, _, heads, hd = k_ref.shape
    d = heads * hd
    k = _dot(u, w_ref[:, :d])
    v = _dot(u, w_ref[:, d:])
    k16_ref[0] = k.astype(BF16)
    v16_ref[0] = v.astype(BF16)
    for hh in range(heads):
        k_ref[0, :, hh, :] = k[:, hh * hd:(hh + 1) * hd]
        v_ref[0, :, hh, :] = v[:, hh * hd:(hh + 1) * hd]


def _mem_kv(mem, g, w):
    b, n, d = mem.shape
    hd = d // MEM_HEADS
    flat = pl.BlockSpec((1, n, d), lambda i: (i, 0, 0))
    out = pl.BlockSpec((1, n, MEM_HEADS, hd), lambda i: (i, 0, 0, 0))
    return pl.pallas_call(
        _mem_kv_body,
        grid=(b,),
        in_specs=[flat, _resident(g.shape), _resident(w.shape)],
        out_specs=[out, out, flat, flat],
        out_shape=[jax.ShapeDtypeStruct((b, n, MEM_HEADS, hd), F32)] * 2 + [jax.ShapeDtypeStruct((b, n, d), BF16)] * 2,
        compiler_params=_params(("parallel",)),
        name="mem_kv",
    )(mem, g, w)


def _mix_out_body(seq, x_ref, og_ref, of_ref, mk_ref, mv_ref, pre_ref, w2_ref, gn_ref, wgo_ref, wfo_ref, wmo_ref,
                  wout_ref, post_ref, o_ref, u_ref):
    x = x_ref[...]
    tm, d = x.shape
    u_ref[...] = _rmsnorm(x, pre_ref[...]).astype(BF16)

    def proj(j):
        return _dot(u_ref[...], w2_ref[:, j * d:(j + 1) * d])

    mq = proj(1)
    hd = d // MEM_HEADS
    scale = float(hd ** -0.5)
    per_batch = []
    for j in range(tm // seq):
        heads = []
        for hh in range(MEM_HEADS):
            sl = slice(hh * hd, (hh + 1) * hd)
            s = _dot_nt(mq[j * seq:(j + 1) * seq, sl].astype(BF16), mk_ref[j, :, sl]) * scale
            e = jnp.exp(s - jnp.max(s, axis=1, keepdims=True))
            p = e / jnp.sum(e, axis=1, keepdims=True)
            heads.append(_dot(p.astype(BF16), mv_ref[j, :, sl]))
        per_batch.append(jnp.concatenate(heads, axis=1))
    o_mem = per_batch[0] if len(per_batch) == 1 else jnp.concatenate(per_batch, axis=0)
    y = _sigmoid(proj(4)) * _dot(o_mem.astype(BF16), wmo_ref[...])

    y = y + _sigmoid(proj(3)) * _dot(of_ref[...], wfo_ref[...])

    og = og_ref[...].astype(F32)
    gd = d // GLA_HEADS
    gn = gn_ref[...]
    normed = []
    for hh in range(GLA_HEADS):
        sl = slice(hh * gd, (hh + 1) * gd)
        normed.append(_rmsnorm(og[:, sl], gn[:, sl]))
    r = proj(0)
    b_gla = _dot((jnp.concatenate(normed, axis=1) * (r * _sigmoid(r))).astype(BF16), wgo_ref[...])
    y = y + _sigmoid(proj(2)) * b_gla

    mixed = _dot(y.astype(BF16), wout_ref[...])
    o_ref[...] = x + _rmsnorm(mixed, post_ref[...])


def _mix_out(x2, o_gla, o_fox, mk, mv, mem_off, seq, pre_g, w2, gn, wgo, wfo, wmo, wout, post_g, tm=512):
    m, d = x2.shape
    if seq < tm:
        nb = 4
        while (m // seq) % nb:
            nb //= 2
        tm = seq * nb
    assert m % tm == 0 and (seq % tm == 0 or tm % seq == 0)
    nb = max(1, tm // seq)
    steps_per_batch = max(1, seq // tm)
    mem = mk.shape[1]
    rows = pl.BlockSpec((tm, d), lambda i: (i, 0))
    assert mem_off % nb == 0
    memb = pl.BlockSpec((nb, mem, d), lambda i: (i // steps_per_batch + mem_off // nb, 0, 0))
    return pl.pallas_call(
        functools.partial(_mix_out_body, min(seq, tm)),
        grid=(m // tm,),
        in_specs=[rows, rows, rows, memb, memb, _resident(pre_g.shape), _resident(w2.shape), _resident(gn.shape),
                  _resident(wgo.shape), _resident(wfo.shape), _resident(wmo.shape), _resident(wout.shape),
                  _resident(post_g.shape)],
        out_specs=rows,
        out_shape=jax.ShapeDtypeStruct((m, d), F32),
        scratch_shapes=[pltpu.VMEM((tm, d), BF16)],
        compiler_params=_params(("parallel",)),
        name="mix_out",
    )(x2, o_gla, o_fox, mk, mv, pre_g, w2, gn, wgo, wfo, wmo, wout, post_g)


def _prep_layer(d, w):
    gk_w = d // 2
    sizes = (gk_w, gk_w, d, GLA_RANK, d, d, d, d, FOX_HEADS, d, N_BRANCH * d)
    offs = np.concatenate([[0], np.cumsum(sizes)]).tolist()
    seg = {n: (offs[i], offs[i + 1]) for i, n in enumerate(
        ("gq", "gk", "gv", "glr", "gr", "fq", "fk", "fv", "ff", "mq", "gates"))}
    w_in = w["w_in"]

    def cols(*names):
        return jnp.concatenate([w_in[:, seg[n][0]:seg[n][1]] for n in names], axis=1).astype(BF16)

    ws = jnp.pad(cols("ff", "glr"), ((0, 0), (0, SMALL_W - FOX_HEADS - GLA_RANK)))
    wa2p = jnp.pad(w["gla_w_a2"].astype(BF16), ((FOX_HEADS, SMALL_W - FOX_HEADS - GLA_RANK), (0, 0)))
    b_f = jnp.pad(w["fox_b_f"], (0, SMALL_W - FOX_HEADS)).reshape(1, SMALL_W)

    def ffn(pfx):
        wg, wu, wd = w[pfx + "_w_gate"], w[pfx + "_w_up"], w[pfx + "_w_down"]
        assert wg.shape[1] % FFN_CHUNK == 0
        return dict(pre=w[pfx + "_pre_g"].reshape(1, d), post=w[pfx + "_post_g"].reshape(1, d),
                    wg=wg.astype(BF16), wu=wu.astype(BF16), wd=wd.astype(BF16))

    return dict(
        ffn1=ffn("ffn1"), ffn2=ffn("ffn2"),
        mix_pre=w["mix_pre_g"].reshape(1, d), mix_post=w["mix_post_g"].reshape(1, d),
        w1=cols("gq", "gk", "gv", "fq", "fk", "fv"), ws=ws, wa2p=wa2p,
        b_a=w["gla_b_a"].reshape(1, gk_w), b_f=b_f,
        w2=cols("gr", "mq", "gates"),
        gn=w["gla_norm_g"].reshape(1, d),
        wgo=w["w_gla_o"].astype(BF16), wfo=w["w_fox_o"].astype(BF16), wmo=w["w_mem_o"].astype(BF16),
        wout=w["w_out"].astype(BF16),
        mem_g=w["mem_norm_g"].reshape(1, d), w_mem_kv=w["w_mem_kv"].astype(BF16),
    )


def _ffn_apply(x2, p):
    return _ffn(x2, p["pre"], p["post"], p["wg"], p["wu"], p["wd"])


def _forget_layouts(f, lead=None):
    b, t, h = f.shape
    if lead is not None:
        f = jnp.concatenate([lead.astype(F32), f], axis=1)
    total = f.shape[1]
    padded = -(-total // LANES) * LANES
    f = jnp.pad(f, ((0, 0), (0, padded - total), (0, 0)))
    rows = _cumsum_time(jnp.swapaxes(f, 1, 2).reshape(b * h, padded // LANES, LANES))
    rows = rows.reshape(b, h, padded)
    cols = jnp.swapaxes(rows[:, :, total - t:total], 1, 2)
    return cols, rows


def _group_layer(x2, b, t, p, layer, depth, carried, gla_state, fox, memory):
    d = x2.shape[1]
    x2 = _ffn_apply(x2, p["ffn1"])
    gq, gk, gv, la, fq, fk16, fv16, *carried = _mix_in(x2, p["mix_pre"], p["w1"], p["ws"], p["wa2p"], p["b_a"],
                                                       p["b_f"], layer, depth, carried)
    r3 = lambda a: a.reshape(b, t, a.shape[1])
    o_gla, s_fin = _gla(r3(gq), r3(gk), r3(gv), r3(la), gla_state[0], gla_state[1])
    f_log = carried[2][layer].reshape(b, t, SMALL_W)[:, :, :FOX_HEADS]
    o_fox = fox(r3(fq), r3(fk16), r3(fv16), f_log)
    x2 = _mix_out(x2, o_gla.reshape(b * t, d), o_fox.reshape(b * t, d), memory[0], memory[1], memory[2], t,
                  p["mix_pre"], p["w2"], p["gn"], p["wgo"], p["wfo"], p["wmo"], p["wout"], p["mix_post"])
    x2 = _ffn_apply(x2, p["ffn2"])
    return x2, carried, s_fin


def kernel(x_prompt, x_sample, cache_fox_k, cache_fox_v, cache_fox_logf, state_gla, cache_mem_k, cache_mem_v, mem_prompt, ffn1_pre_g, ffn1_post_g, ffn1_w_gate, ffn1_w_up, ffn1_w_down, mix_pre_g, mix_post_g, w_in, gla_w_a2, gla_b_a, fox_b_f, gla_norm_g, w_gla_o, w_fox_o, w_mem_o, w_out, mem_norm_g, w_mem_kv, ffn2_pre_g, ffn2_post_g, ffn2_w_gate, ffn2_w_up, ffn2_w_down):
    weights = dict(ffn1_pre_g=ffn1_pre_g, ffn1_post_g=ffn1_post_g, ffn1_w_gate=ffn1_w_gate, ffn1_w_up=ffn1_w_up,
                   ffn1_w_down=ffn1_w_down, mix_pre_g=mix_pre_g, mix_post_g=mix_post_g, w_in=w_in,
                   gla_w_a2=gla_w_a2, gla_b_a=gla_b_a, fox_b_f=fox_b_f, gla_norm_g=gla_norm_g, w_gla_o=w_gla_o,
                   w_fox_o=w_fox_o, w_mem_o=w_mem_o, w_out=w_out, mem_norm_g=mem_norm_g, w_mem_kv=w_mem_kv,
                   ffn2_pre_g=ffn2_pre_g, ffn2_post_g=ffn2_post_g, ffn2_w_gate=ffn2_w_gate, ffn2_w_up=ffn2_w_up,
                   ffn2_w_down=ffn2_w_down)
    depth = w_in.shape[0]
    bp, tp, d = x_prompt.shape
    bs, ts, _ = x_sample.shape
    hd = d // FOX_HEADS
    dk, dv = state_gla.shape[-2:]
    layers = [_prep_layer(d, {k: v[l] for k, v in weights.items()}) for l in range(depth)]

    x = x_prompt.reshape(bp * tp, d)
    zero_state = jnp.zeros((bp, GLA_HEADS, dk, dv), F32)
    tq = min(512, tp)

    def fox_p(fq, fk, fv, f_log):
        f_cols, _ = _forget_layouts(f_log)
        return _fox_prompt(_fox_augment(fq, f_cols, True), _fox_augment(fk, f_cols, False), fv, tq=tq)

    carried, p_state, p_mk, p_mv = [], [], [], []
    for l in range(depth):
        mk, mv, mk16, mv16 = _mem_kv(mem_prompt, layers[l]["mem_g"], layers[l]["w_mem_kv"])
        x, carried, s_fin = _group_layer(x, bp, tp, layers[l], l, depth, carried, (zero_state, 0), fox_p,
                                         (mk16, mv16, 0))
        p_state.append(s_fin)
        p_mk.append(mk)
        p_mv.append(mv)
    y_prompt = x.reshape(bp, tp, d)
    p_fox_k = carried[0].reshape(depth, bp, tp, FOX_HEADS, hd)
    p_fox_v = carried[1].reshape(depth, bp, tp, FOX_HEADS, hd)
    p_fox_logf = carried[2].reshape(depth, bp, tp, SMALL_W)[..., :FOX_HEADS]
    p_mem_k = jnp.stack(p_mk)
    p_mem_v = jnp.stack(p_mv)

    x = x_sample.reshape(bs * ts, d)
    past = cache_fox_k.shape[2]
    ck = cache_fox_k.reshape(depth * bs, past, FOX_HEADS, hd)
    cv = cache_fox_v.reshape(depth * bs, past, FOX_HEADS, hd)
    states = state_gla.reshape(depth * bs, GLA_HEADS, dk, dv)
    mem_s = cache_mem_k.shape[2]
    cmk = cache_mem_k.reshape(depth * bs, mem_s, d).astype(BF16)
    cmv = cache_mem_v.reshape(depth * bs, mem_s, d).astype(BF16)
    carried, s_state = [], []
    for l in range(depth):
        def fox_s(fq, fk, fv, f_log, l=l):
            _, f_rows = _forget_layouts(f_log, lead=cache_fox_logf[l])
            f_all = jnp.swapaxes(f_rows[:, :, :past + ts], 1, 2)
            return _fox_sample(fq, fk, fv, ck, cv, l * bs, f_all)

        x, carried, s_new = _group_layer(x, bs, ts, layers[l], l, depth, carried, (states, l * bs), fox_s,
                                         (cmk, cmv, l * bs))
        s_state.append(s_new)
    y_sample = x.reshape(bs, ts, d)
    s_fox_k = carried[0].reshape(depth, bs, ts, FOX_HEADS, hd)
    s_fox_v = carried[1].reshape(depth, bs, ts, FOX_HEADS, hd)
    s_fox_logf = carried[2].reshape(depth, bs, ts, SMALL_W)[..., :FOX_HEADS]

    return (y_prompt, y_sample, p_fox_k, p_fox_v, p_fox_logf, jnp.stack(p_state), p_mem_k, p_mem_v,
            s_fox_k, s_fox_v, s_fox_logf, jnp.stack(s_state))
```

```python
import functools

import numpy as np
import jax
import jax.numpy as jnp
from jax import lax
from jax.experimental import pallas as pl
from jax.experimental.pallas import tpu as pltpu

F32 = jnp.float32
BF16 = jnp.bfloat16

EPS = 1e-6
GLA_HEADS = 4
GLA_RANK = 16
GLA_TAU = 16.0
FOX_HEADS = 8
MEM_HEADS = 4
N_BRANCH = 3

LANES = 128
SUBLANES = 8
V7X_VMEM_LIMIT = 56 * 1024 * 1024
FFN_CHUNK = 256
GLA_CHUNK = 64
GLA_SUB = 16
FOX_HEADS_PER_STEP = 4
FOX_WIDE = 2
LOG2E = 1.4426950408889634
SMALL_W = 128


def _dot(a, b):
    return jnp.dot(a, b, preferred_element_type=F32)


def _dot_nt(a, b):
    return lax.dot_general(a, b, (((1,), (1,)), ((), ())), preferred_element_type=F32)


def _dot_tn(a, b):
    return lax.dot_general(a, b, (((0,), (0,)), ((), ())), preferred_element_type=F32)


def _rmsnorm(x, g):
    return x * lax.rsqrt(jnp.mean(x * x, axis=-1, keepdims=True) + EPS) * g


def _sigmoid(x):
    return 1.0 / (1.0 + jnp.exp(-x))


def _log_sigmoid(x):
    return jnp.minimum(x, 0.0) - jnp.log1p(jnp.exp(-jnp.abs(x)))


def _split3(x):
    hi = x.astype(BF16)
    r1 = x - hi.astype(F32)
    mid = r1.astype(BF16)
    lo = (r1 - mid.astype(F32)).astype(BF16)
    return hi, mid, lo


def _resident(shape):
    nd = len(shape)
    return pl.BlockSpec(shape, lambda *_: (0,) * nd, pipeline_mode=pl.Buffered(1))


def _params(semantics):
    return pltpu.CompilerParams(dimension_semantics=semantics, vmem_limit_bytes=V7X_VMEM_LIMIT)


def _row_tile(m, want):
    t = min(m, want)
    assert m % t == 0, (m, t)
    return t


def _ffn_body(x_ref, pre_ref, post_ref, wg_ref, wu_ref, wd_ref, o_ref, u_ref, acc_ref):
    x = x_ref[...]
    u_ref[...] = _rmsnorm(x, pre_ref[...]).astype(BF16)
    for c in range(wg_ref.shape[1] // FFN_CHUNK):
        cols = slice(c * FFN_CHUNK, (c + 1) * FFN_CHUNK)
        u = u_ref[...]
        g = _dot(u, wg_ref[:, cols])
        up = _dot(u, wu_ref[:, cols])
        h = (g * _sigmoid(g) * up).astype(BF16)
        part = _dot(h, wd_ref[cols, :])
        if c == 0:
            acc_ref[...] = part
        else:
            acc_ref[...] += part
    o_ref[...] = x + 0.5 * _rmsnorm(acc_ref[...], post_ref[...])


def _ffn(x2, pre_g, post_g, wg, wu, wd, tm=512):
    m, d = x2.shape
    tm = _row_tile(m, tm)
    return pl.pallas_call(
        _ffn_body,
        grid=(m // tm,),
        in_specs=[pl.BlockSpec((tm, d), lambda i: (i, 0)),
                  _resident(pre_g.shape), _resident(post_g.shape),
                  _resident(wg.shape), _resident(wu.shape), _resident(wd.shape)],
        out_specs=pl.BlockSpec((tm, d), lambda i: (i, 0)),
        out_shape=jax.ShapeDtypeStruct((m, d), F32),
        scratch_shapes=[pltpu.VMEM((tm, d), BF16), pltpu.VMEM((tm, d), F32)],
        compiler_params=_params(("parallel",)),
        name="ffn",
    )(x2, pre_g, post_g, wg, wu, wd)


def _mix_in_body(dims, n_carried, tiles_per_seq, x_ref, pre_ref, w_ref, ws_ref, wa2_ref, ba_ref, bf_ref, *refs):
    augment = tiles_per_seq is not None
    if augment:
        place_q_ref, place_k_ref, ones_q_ref, ones_k_ref = refs[:4]
        refs = refs[4:]
    refs = refs[n_carried:]
    gq_ref, gk_ref, gv_ref, la_ref, fq_ref, fk16_ref, fv16_ref, fk32_ref, fv32_ref, fl_ref, u_ref = refs[:11]
    gk_w, gv_w, fx_w, dk_scale, hd_scale = dims
    u_ref[...] = _rmsnorm(x_ref[...], pre_ref[...]).astype(BF16)

    def proj(lo, width):
        return _dot(u_ref[...], w_ref[:, lo:lo + width])

    off = 0
    gq_ref[...] = proj(off, gk_w) * dk_scale
    off += gk_w
    gk_ref[...] = proj(off, gk_w)
    off += gk_w
    gv_ref[...] = proj(off, gv_w).astype(BF16)
    off += gv_w
    fq16 = (proj(off, fx_w) * hd_scale).astype(BF16)
    off += fx_w
    hd = fx_w // FOX_HEADS
    fk = proj(off, fx_w)
    fk16 = fk.astype(BF16)
    off += fx_w
    fv = proj(off, fx_w)
    fv16_ref[...] = fv.astype(BF16)
    for hh in range(FOX_HEADS):
        fk32_ref[:, hh, :] = fk[:, hh * hd:(hh + 1) * hd]
        fv32_ref[:, hh, :] = fv[:, hh * hd:(hh + 1) * hd]

    small = _dot(u_ref[...], ws_ref[...])
    lane = lax.broadcasted_iota(jnp.int32, small.shape, 1)
    log_f = _log_sigmoid(small + bf_ref[...])
    fl_ref[...] = jnp.where(lane < FOX_HEADS, log_f, 0.0)
    lr = jnp.where((lane >= FOX_HEADS) & (lane < FOX_HEADS + GLA_RANK), small, 0.0).astype(BF16)
    la_ref[...] = _log_sigmoid(_dot(lr, wa2_ref[...]) + ba_ref[...]) * (1.0 / GLA_TAU)

    if not augment:
        fq_ref[...] = fq16
        fk16_ref[...] = fk16
        return

    f_ref, carry_ref = refs[11:]
    tm = small.shape[0]
    copies = FOX_HEADS + GLA_RANK
    is_f = (lane < FOX_HEADS) | ((lane >= copies) & (lane < copies + 2 * FOX_HEADS))
    src = jnp.where(is_f, log_f, 0.0)

    @pl.when(pl.program_id(0) % tiles_per_seq == 0)
    def _():
        carry_ref[...] = jnp.zeros_like(carry_ref)

    r = lax.broadcasted_iota(jnp.int32, (LANES, LANES), 0)
    c = lax.broadcasted_iota(jnp.int32, (LANES, LANES), 1)
    lower = jnp.where(r >= c, 1.0, 0.0).astype(BF16)
    for j in range(tm // LANES):
        rows = slice(j * LANES, (j + 1) * LANES)
        hi, mid, lo = _split3(src[rows])
        f_ref[rows, :] = _dot(lower, hi) + _dot(lower, mid) + _dot(lower, lo)
        f_ref[rows, :] = f_ref[rows, :] + carry_ref[...]
        carry_ref[...] = f_ref[(j + 1) * LANES - 1:(j + 1) * LANES, :]
    hi, mid, lo = _split3(f_ref[...] * LOG2E)
    pieces = jnp.where(lane < FOX_HEADS, hi, jnp.where(lane < copies + FOX_HEADS, mid, lo))
    extra_q = (_dot(pieces, place_q_ref[...]) + ones_q_ref[...]).astype(BF16)
    extra_k = (_dot(pieces, place_k_ref[...]) + ones_k_ref[...]).astype(BF16)
    q_parts, k_parts = [], []
    for hh in range(FOX_HEADS):
        sl = slice(hh * hd, (hh + 1) * hd)
        q_parts += [fq16[:, sl], extra_q[:, sl]]
        k_parts += [fk16[:, sl], extra_k[:, sl]]
    fq_ref[...] = jnp.concatenate(q_parts, axis=1)
    fk16_ref[...] = jnp.concatenate(k_parts, axis=1)


def _augment_constants(d):
    hd = d // FOX_HEADS
    copies = FOX_HEADS + GLA_RANK
    place_q = np.zeros((SMALL_W, d), np.float32)
    place_k = np.zeros((SMALL_W, d), np.float32)
    ones_q = np.zeros((1, d), np.float32)
    ones_k = np.zeros((1, d), np.float32)
    for hh in range(FOX_HEADS):
        for j, lane in enumerate((hh, copies + hh, copies + FOX_HEADS + hh)):
            place_q[lane, hh * hd + j] = 1.0
            place_k[lane, hh * hd + 3 + j] = -1.0
            ones_q[0, hh * hd + 3 + j] = 1.0
            ones_k[0, hh * hd + j] = 1.0
    return jnp.asarray(place_q, BF16), jnp.asarray(place_k, BF16), jnp.asarray(ones_q), jnp.asarray(ones_k)


def _mix_in(x2, pre_g, w1, ws, wa2p, b_a, b_f, layer, depth, carried, seq=None, tm=512):
    m, d = x2.shape
    tm = _row_tile(m, tm)
    gk_w = d // 2
    dims = (gk_w, d, d, float((gk_w // GLA_HEADS) ** -0.5), float((d // FOX_HEADS) ** -0.5) * LOG2E)
    augment = seq is not None
    assert not augment or (seq % tm == 0 and tm % LANES == 0)

    def rows(width):
        return pl.BlockSpec((tm, width), lambda i: (i, 0))

    def stacked(*tail):
        return pl.BlockSpec((None, tm) + tail, lambda i: (layer, i) + (0,) * len(tail))

    qk_w = 2 * d if augment else d
    outs = [(gk_w, F32), (gk_w, F32), (d, BF16), (gk_w, F32), (qk_w, BF16), (qk_w, BF16), (d, BF16)]
    stacks = [(FOX_HEADS, d // FOX_HEADS), (FOX_HEADS, d // FOX_HEADS), (SMALL_W,)]
    consts = _augment_constants(d) if augment else ()
    n_in = 7 + len(consts)
    scratch = [pltpu.VMEM((tm, d), BF16)]
    if augment:
        scratch += [pltpu.VMEM((tm, SMALL_W), F32), pltpu.VMEM((1, SMALL_W), F32)]
    return pl.pallas_call(
        functools.partial(_mix_in_body, dims, len(carried), seq // tm if augment else None),
        grid=(m // tm,),
        in_specs=[rows(d), _resident(pre_g.shape), _resident(w1.shape), _resident(ws.shape),
                  _resident(wa2p.shape), _resident(b_a.shape), _resident(b_f.shape)]
                 + [_resident(c.shape) for c in consts]
                 + [pl.BlockSpec(memory_space=pl.ANY)] * len(carried),
        out_specs=[rows(w) for w, _ in outs] + [stacked(*tail) for tail in stacks],
        out_shape=[jax.ShapeDtypeStruct((m, w), dt) for w, dt in outs]
                  + [jax.ShapeDtypeStruct((depth, m) + tail, F32) for tail in stacks],
        input_output_aliases={n_in + j: len(outs) + j for j in range(len(carried))},
        scratch_shapes=scratch,
        compiler_params=_params(("arbitrary",) if augment else ("parallel",)),
        name="mix_in",
    )(x2, pre_g, w1, ws, wa2p, b_a, b_f, *consts, *carried)


def _cumsum_body(x_ref, m_ref, o_ref):
    x = x_ref[...]
    r = lax.broadcasted_iota(jnp.int32, (LANES, LANES), 0)
    c = lax.broadcasted_iota(jnp.int32, (LANES, LANES), 1)
    upper = jnp.where(r <= c, 1.0, 0.0).astype(BF16)
    hi, mid, lo = _split3(x)
    o_ref[...] = _dot(hi, upper) + _dot(mid, upper) + _dot(lo, upper)
    tot = jnp.broadcast_to(o_ref[:, LANES - 1:LANES], x.shape)
    hi, mid, lo = _split3(tot)
    mm = m_ref[...]
    offset = _dot(mm, hi) + _dot(mm, mid) + _dot(mm, lo)
    o_ref[...] = o_ref[...] + offset


def _cumsum_time(x3):
    g, n, _ = x3.shape
    per = 8
    while (per * n) % 8:
        per += 1
    gpb = per if g % per == 0 else g
    rb = gpb * n
    idx = np.arange(rb)
    earlier = (idx[:, None] // n == idx[None, :] // n) & (idx[None, :] < idx[:, None])
    mm = jnp.asarray(earlier.astype(np.float32), BF16)
    out = pl.pallas_call(
        _cumsum_body,
        grid=(g // gpb,),
        in_specs=[pl.BlockSpec((rb, LANES), lambda i: (i, 0)), _resident((rb, rb))],
        out_specs=pl.BlockSpec((rb, LANES), lambda i: (i, 0)),
        out_shape=jax.ShapeDtypeStruct((g * n, LANES), F32),
        compiler_params=_params(("parallel",)),
        name="cumsum",
    )(x3.reshape(g * n, LANES), mm)
    return out.reshape(g, n, LANES)


def _gla_chunk(q, k, v, la, st):
    cl, dk = q.shape
    sub = min(GLA_SUB, cl)
    row = lax.broadcasted_iota(jnp.int32, (cl, cl), 0)
    col = lax.broadcasted_iota(jnp.int32, (cl, cl), 1)
    lower = jnp.where(row >= col, 1.0, 0.0).astype(BF16)
    hi, mid, lo = _split3(la)
    b = _dot(lower, hi) + _dot(lower, mid) + _dot(lower, lo)
    b_end = b[cl - 1:cl, :]

    o = _dot_nt((q * jnp.exp(b)).astype(BF16), st.astype(BF16))

    tsub = lax.broadcasted_iota(jnp.int32, (SUBLANES, dk), 0)
    acol = lax.broadcasted_iota(jnp.int32, (SUBLANES, cl), 1)
    blocks = []
    for i in range(cl // sub):
        r0 = i * sub
        b_i, q_i, k_i = b[r0:r0 + sub], q[r0:r0 + sub], k[r0:r0 + sub]
        groups = [jnp.zeros((SUBLANES, cl), F32) for _ in range(sub // SUBLANES)]
        for s in range(sub):
            own = s // SUBLANES
            for gi in range(own, sub // SUBLANES):
                rows = slice(gi * SUBLANES, (gi + 1) * SUBLANES)
                rel = b_i[rows] - b_i[s:s + 1]
                if gi == own:
                    rel = jnp.where(tsub >= s - gi * SUBLANES, rel, -jnp.inf)
                w = jnp.sum(q_i[rows] * k_i[s:s + 1] * jnp.exp(rel), axis=1, keepdims=True)
                groups[gi] = jnp.where(acol == r0 + s, w, groups[gi])
        a_i = jnp.concatenate(groups, axis=0)
        if i > 0:
            ref = b[r0 - 1:r0]
            q_t = (q_i * jnp.exp(b_i - ref)).astype(BF16)
            k_t = jnp.concatenate([k[:r0] * jnp.exp(ref - b[:r0]), jnp.zeros((cl - r0, dk), F32)], axis=0)
            a_i = a_i + _dot_nt(q_t, k_t.astype(BF16))
        blocks.append(a_i)
    attn = jnp.concatenate(blocks, axis=0).astype(BF16)
    o = o + _dot(attn, v)

    k_d = (k * jnp.exp(b_end - b)).astype(BF16)
    st_new = st * jnp.exp(b_end) + _dot_tn(v, k_d)
    return o, st_new


def _gla_body(cl, q_ref, k_ref, v_ref, la_ref, s0_ref, o_ref, sf_ref, st_ref):
    t = pl.program_id(1)
    heads, dv, dk = st_ref.shape

    @pl.when(t == 0)
    def _():
        for hh in range(heads):
            st_ref[hh] = s0_ref[0, hh].T

    n_chunks = q_ref.shape[1] // cl

    def chunk(ci, carry):
        r = pl.multiple_of(ci * cl, cl)
        for hh in range(heads):
            ks = slice(hh * dk, (hh + 1) * dk)
            vs = slice(hh * dv, (hh + 1) * dv)
            o, st = _gla_chunk(q_ref[0, pl.ds(r, cl), ks], k_ref[0, pl.ds(r, cl), ks], v_ref[0, pl.ds(r, cl), vs],
                               la_ref[0, pl.ds(r, cl), ks], st_ref[hh])
            o_ref[0, pl.ds(r, cl), vs] = o.astype(o_ref.dtype)
            st_ref[hh] = st
        return carry

    lax.fori_loop(0, n_chunks, chunk, 0, unroll=2 if n_chunks % 2 == 0 else 1)

    @pl.when(t == pl.num_programs(1) - 1)
    def _():
        for hh in range(heads):
            sf_ref[0, hh] = st_ref[hh].T


def _gla(gq, gk, gv, la, s0, s0_off=0, tt=512):
    b, t, _ = gq.shape
    _, h, dk, dv = s0.shape
    cl = min(GLA_CHUNK, t)
    tt = _row_tile(t, tt)
    return pl.pallas_call(
        functools.partial(_gla_body, cl),
        grid=(b, t // tt),
        in_specs=[pl.BlockSpec((1, tt, h * dk), lambda i, n: (i, n, 0)),
                  pl.BlockSpec((1, tt, h * dk), lambda i, n: (i, n, 0)),
                  pl.BlockSpec((1, tt, h * dv), lambda i, n: (i, n, 0)),
                  pl.BlockSpec((1, tt, h * dk), lambda i, n: (i, n, 0)),
                  pl.BlockSpec((1, h, dk, dv), lambda i, n: (i + s0_off, 0, 0, 0))],
        out_specs=[pl.BlockSpec((1, tt, h * dv), lambda i, n: (i, n, 0)),
                   pl.BlockSpec((1, h, dk, dv), lambda i, n: (i, 0, 0, 0))],
        out_shape=[jax.ShapeDtypeStruct((b, t, h * dv), BF16), jax.ShapeDtypeStruct((b, h, dk, dv), F32)],
        scratch_shapes=[pltpu.VMEM((h, dv, dk), F32)],
        compiler_params=_params(("parallel", "arbitrary")),
        name="gla",
    )(gq, gk, gv, la, s0)


def _fox_prompt_body(hps, q_ref, k_ref, v_ref, o_ref, m_ref, acc_ref):
    qi = pl.program_id(2)
    tq = q_ref.shape[1]
    hw = q_ref.shape[2] // hps
    hd = v_ref.shape[2] // hps
    wide = FOX_WIDE * tq

    def logits(hh, r, width):
        sl = slice(hh * hw, (hh + 1) * hw)
        return _dot_nt(q_ref[0, :, sl], k_ref[0, pl.ds(r, width), sl])

    def values(hh, r, width):
        lane = lax.broadcasted_iota(jnp.int32, (width, LANES), 1)
        ones_col = jnp.where(lane == 0, 1.0, 0.0).astype(BF16)
        return jnp.concatenate([v_ref[0, pl.ds(r, width), hh * hd:(hh + 1) * hd], ones_col], axis=1)

    def first(r, width):
        row = lax.broadcasted_iota(jnp.int32, (tq, width), 0)
        col = lax.broadcasted_iota(jnp.int32, (tq, width), 1)
        for hh in range(hps):
            z = jnp.where(col <= row + (width - tq), logits(hh, r, width), -jnp.inf)
            m = jnp.max(z, axis=1, keepdims=True)
            m_ref[hh] = m
            acc_ref[hh] = _dot(jnp.exp2(z - m).astype(BF16), values(hh, r, width))

    assert FOX_WIDE == 2
    lead = (qi + 1) % FOX_WIDE

    @pl.when(lead == 1)
    def _():
        first(pl.multiple_of(qi * tq, tq), tq)

    @pl.when(lead == 0)
    def _():
        first(pl.multiple_of((qi + 1) * tq - wide, tq), wide)

    def update(r, width):
        for hh in range(hps):
            z = logits(hh, r, width)
            m_old = m_ref[hh]
            m_new = jnp.maximum(m_old, jnp.max(z, axis=1, keepdims=True))
            m_ref[hh] = m_new
            pv = _dot(jnp.exp2(z - m_new).astype(BF16), values(hh, r, width))
            acc_ref[hh] = jnp.exp2(m_old - m_new) * acc_ref[hh] + pv

    n_wide = qi // FOX_WIDE

    @pl.when(n_wide % 2 == 1)
    def _():
        update(0, wide)

    base = (n_wide % 2) * wide

    def double_wide_block(j, carry):
        update(pl.multiple_of(base + j * 2 * wide, wide), 2 * wide)
        return carry

    lax.fori_loop(0, n_wide // 2, double_wide_block, 0)
    for hh in range(hps):
        acc = acc_ref[hh]
        o_ref[0, :, hh * hd:(hh + 1) * hd] = (acc[:, :hd] / acc[:, hd:hd + 1]).astype(o_ref.dtype)


def _fox_prompt(q_aug, k_aug, fv, tq=512, hps=FOX_HEADS_PER_STEP):
    b, t, d = fv.shape
    h = FOX_HEADS
    hd = d // h
    nq = t // tq
    keys = pl.BlockSpec((1, t, hps * 2 * hd), lambda i, j, n: (i, 0, j), pipeline_mode=pl.Buffered(1))
    vals = pl.BlockSpec((1, t, hps * hd), lambda i, j, n: (i, 0, j), pipeline_mode=pl.Buffered(1))
    return pl.pallas_call(
        functools.partial(_fox_prompt_body, hps),
        grid=(b, h // hps, nq),
        in_specs=[pl.BlockSpec((1, tq, hps * 2 * hd), lambda i, j, n: (i, n, j)), keys, vals],
        out_specs=pl.BlockSpec((1, tq, hps * hd), lambda i, j, n: (i, n, j)),
        out_shape=jax.ShapeDtypeStruct((b, t, d), BF16),
        scratch_shapes=[pltpu.VMEM((hps, tq, 1), F32), pltpu.VMEM((hps, tq, hd + LANES), F32)],
        compiler_params=_params(("parallel", "parallel", "arbitrary")),
        name="fox_prompt",
    )(q_aug, k_aug, fv)


def _fox_sample_body(heads, q_ref, kc_ref, vc_ref, kn_ref, vn_ref, fq_ref, fkc_ref, fkn_ref, o_ref,
                     m_ref, l_ref, acc_ref):
    c = pl.program_id(1)
    q = q_ref[0]
    rows = q.shape[0]
    ln = rows // heads
    f_q = fq_ref[0] * LOG2E
    head_q = lax.broadcasted_iota(jnp.int32, (rows, 1), 0) // ln

    def head_k(n):
        return lax.broadcasted_iota(jnp.int32, (1, n), 1) % heads

    @pl.when(c == 0)
    def _():
        n = kn_ref.shape[1]
        z = _dot_nt(q, kn_ref[0]) + f_q - fkn_ref[0] * LOG2E
        step_k = lax.broadcasted_iota(jnp.int32, (1, n), 1) // heads
        step_q = lax.broadcasted_iota(jnp.int32, (rows, 1), 0) % ln
        z = jnp.where((head_k(n) == head_q) & (step_k <= step_q), z, -jnp.inf)
        m = jnp.max(z, axis=1, keepdims=True)
        p = jnp.exp2(z - m)
        m_ref[...] = m
        l_ref[...] = jnp.sum(p, axis=1, keepdims=True)
        acc_ref[...] = _dot(p.astype(BF16), vn_ref[0])

    n = kc_ref.shape[1]
    z = _dot_nt(q, kc_ref[0].astype(BF16)) + f_q - fkc_ref[0] * LOG2E
    z = jnp.where(head_k(n) == head_q, z, -jnp.inf)
    m_old = m_ref[...]
    m_new = jnp.maximum(m_old, jnp.max(z, axis=1, keepdims=True))
    alpha = jnp.exp2(m_old - m_new)
    p = jnp.exp2(z - m_new)
    m_ref[...] = m_new
    l_ref[...] = alpha * l_ref[...] + jnp.sum(p, axis=1, keepdims=True)
    acc_ref[...] = alpha * acc_ref[...] + _dot(p.astype(BF16), vc_ref[0].astype(BF16))

    @pl.when(c == pl.num_programs(1) - 1)
    def _():
        o_ref[0] = (acc_ref[...] / l_ref[...]).astype(o_ref.dtype)


def _fox_sample(fq, fk, fv, cache_k, cache_v, cache_off, f_all, chunk_steps=512):
    b, ln, d = fq.shape
    rows_total, past, h, hd = cache_k.shape
    assert h & (h - 1) == 0 and ln & (ln - 1) == 0
    chunk_steps = _row_tile(past, chunk_steps)
    ch = chunk_steps * h
    q_rows = fq.reshape(b, ln, h, hd).transpose(0, 2, 1, 3).reshape(b, h * ln, hd)
    f_q = jnp.swapaxes(f_all[:, past:, :], 1, 2).reshape(b, h * ln, 1)
    f_k = f_all.reshape(b, 1, (past + ln) * h)
    new = pl.BlockSpec((1, ln * h, hd), lambda i, c: (i, 0, 0))
    cache = pl.BlockSpec((1, ch, hd), lambda i, c: (i + cache_off, c, 0))
    o = pl.pallas_call(
        functools.partial(_fox_sample_body, h),
        grid=(b, past // chunk_steps),
        in_specs=[new, cache, cache, new, new,
                  pl.BlockSpec((1, h * ln, 1), lambda i, c: (i, 0, 0)),
                  pl.BlockSpec((1, 1, ch), lambda i, c: (i, 0, c)),
                  pl.BlockSpec((1, 1, ln * h), lambda i, c: (i, 0, 0))],
        out_specs=new,
        out_shape=jax.ShapeDtypeStruct((b, h * ln, hd), BF16),
        scratch_shapes=[pltpu.VMEM((h * ln, 1), F32), pltpu.VMEM((h * ln, 1), F32), pltpu.VMEM((h * ln, hd), F32)],
        compiler_params=_params(("parallel", "arbitrary")),
        name="fox_sample",
    )(q_rows, cache_k.reshape(rows_total, past * h, hd), cache_v.reshape(rows_total, past * h, hd),
      fk.reshape(b, ln * h, hd), fv.reshape(b, ln * h, hd), f_q, f_k[:, :, :past * h], f_k[:, :, past * h:])
    return o.reshape(b, h, ln, hd).transpose(0, 2, 1, 3).reshape(b, ln, d)


def _mem_kv_body(mem_ref, g_ref, w_ref, k_ref, v_ref, k16_ref, v16_ref):
    u = _rmsnorm(mem_ref[0], g_ref[...]).astype(BF16)
    _, _, heads, hd = k_ref.shape
    d = heads * hd
    k = _dot(u, w_ref[:, :d])
    v = _dot(u, w_ref[:, d:])
    k16_ref[0] = k.astype(BF16)
    v16_ref[0] = v.astype(BF16)
    for hh in range(heads):
        k_ref[0, :, hh, :] = k[:, hh * hd:(hh + 1) * hd]
        v_ref[0, :, hh, :] = v[:, hh * hd:(hh + 1) * hd]


def _mem_kv(mem, g, w):
    b, n, d = mem.shape
    hd = d // MEM_HEADS
    flat = pl.BlockSpec((1, n, d), lambda i: (i, 0, 0))
    out = pl.BlockSpec((1, n, MEM_HEADS, hd), lambda i: (i, 0, 0, 0))
    return pl.pallas_call(
        _mem_kv_body,
        grid=(b,),
        in_specs=[flat, _resident(g.shape), _resident(w.shape)],
        out_specs=[out, out, flat, flat],
        out_shape=[jax.ShapeDtypeStruct((b, n, MEM_HEADS, hd), F32)] * 2 + [jax.ShapeDtypeStruct((b, n, d), BF16)] * 2,
        compiler_params=_params(("parallel",)),
        name="mem_kv",
    )(mem, g, w)


def _mix_out_body(seq, x_ref, og_ref, of_ref, mk_ref, mv_ref, pre_ref, w2_ref, gn_ref, wgo_ref, wfo_ref, wmo_ref,
                  wout_ref, post_ref, o_ref, u_ref):
    x = x_ref[...]
    tm, d = x.shape
    u_ref[...] = _rmsnorm(x, pre_ref[...]).astype(BF16)

    def proj(j):
        return _dot(u_ref[...], w2_ref[:, j * d:(j + 1) * d])

    mq = proj(1)
    hd = d // MEM_HEADS
    scale = float(hd ** -0.5)
    per_batch = []
    for j in range(tm // seq):
        heads = []
        for hh in range(MEM_HEADS):
            sl = slice(hh * hd, (hh + 1) * hd)
            s = _dot_nt(mq[j * seq:(j + 1) * seq, sl].astype(BF16), mk_ref[j, :, sl]) * scale
            e = jnp.exp(s - jnp.max(s, axis=1, keepdims=True))
            p = e / jnp.sum(e, axis=1, keepdims=True)
            heads.append(_dot(p.astype(BF16), mv_ref[j, :, sl]))
        per_batch.append(jnp.concatenate(heads, axis=1))
    o_mem = per_batch[0] if len(per_batch) == 1 else jnp.concatenate(per_batch, axis=0)
    y = _sigmoid(proj(4)) * _dot(o_mem.astype(BF16), wmo_ref[...])

    y = y + _sigmoid(proj(3)) * _dot(of_ref[...], wfo_ref[...])

    og = og_ref[...].astype(F32)
    gd = d // GLA_HEADS
    gn = gn_ref[...]
    normed = []
    for hh in range(GLA_HEADS):
        sl = slice(hh * gd, (hh + 1) * gd)
        normed.append(_rmsnorm(og[:, sl], gn[:, sl]))
    r = proj(0)
    b_gla = _dot((jnp.concatenate(normed, axis=1) * (r * _sigmoid(r))).astype(BF16), wgo_ref[...])
    y = y + _sigmoid(proj(2)) * b_gla

    mixed = _dot(y.astype(BF16), wout_ref[...])
    o_ref[...] = x + _rmsnorm(mixed, post_ref[...])


def _mix_out(x2, o_gla, o_fox, mk, mv, mem_off, seq, pre_g, w2, gn, wgo, wfo, wmo, wout, post_g, tm=512):
    m, d = x2.shape
    if seq < tm:
        nb = 4
        while (m // seq) % nb:
            nb //= 2
        tm = seq * nb
    assert m % tm == 0 and (seq % tm == 0 or tm % seq == 0)
    nb = max(1, tm // seq)
    steps_per_batch = max(1, seq // tm)
    mem = mk.shape[1]
    rows = pl.BlockSpec((tm, d), lambda i: (i, 0))
    assert mem_off % nb == 0
    memb = pl.BlockSpec((nb, mem, d), lambda i: (i // steps_per_batch + mem_off // nb, 0, 0))
    return pl.pallas_call(
        functools.partial(_mix_out_body, min(seq, tm)),
        grid=(m // tm,),
        in_specs=[rows, rows, rows, memb, memb, _resident(pre_g.shape), _resident(w2.shape), _resident(gn.shape),
                  _resident(wgo.shape), _resident(wfo.shape), _resident(wmo.shape), _resident(wout.shape),
                  _resident(post_g.shape)],
        out_specs=rows,
        out_shape=jax.ShapeDtypeStruct((m, d), F32),
        scratch_shapes=[pltpu.VMEM((tm, d), BF16)],
        compiler_params=_params(("parallel",)),
        name="mix_out",
    )(x2, o_gla, o_fox, mk, mv, pre_g, w2, gn, wgo, wfo, wmo, wout, post_g)


def _prep_layer(d, w):
    gk_w = d // 2
    sizes = (gk_w, gk_w, d, GLA_RANK, d, d, d, d, FOX_HEADS, d, N_BRANCH * d)
    offs = np.concatenate([[0], np.cumsum(sizes)]).tolist()
    seg = {n: (offs[i], offs[i + 1]) for i, n in enumerate(
        ("gq", "gk", "gv", "glr", "gr", "fq", "fk", "fv", "ff", "mq", "gates"))}
    w_in = w["w_in"]

    def cols(*names):
        return jnp.concatenate([w_in[:, seg[n][0]:seg[n][1]] for n in names], axis=1).astype(BF16)

    ws = jnp.pad(cols("ff", "glr", "ff", "ff"), ((0, 0), (0, SMALL_W - 3 * FOX_HEADS - GLA_RANK)))
    wa2p = jnp.pad(w["gla_w_a2"].astype(BF16), ((FOX_HEADS, SMALL_W - FOX_HEADS - GLA_RANK), (0, 0)))
    b_f = jnp.concatenate([w["fox_b_f"], jnp.zeros((GLA_RANK,), F32), w["fox_b_f"], w["fox_b_f"]])
    b_f = jnp.pad(b_f, (0, SMALL_W - b_f.shape[0])).reshape(1, SMALL_W)

    def ffn(pfx):
        wg, wu, wd = w[pfx + "_w_gate"], w[pfx + "_w_up"], w[pfx + "_w_down"]
        assert wg.shape[1] % FFN_CHUNK == 0
        return dict(pre=w[pfx + "_pre_g"].reshape(1, d), post=w[pfx + "_post_g"].reshape(1, d),
                    wg=wg.astype(BF16), wu=wu.astype(BF16), wd=wd.astype(BF16))

    return dict(
        ffn1=ffn("ffn1"), ffn2=ffn("ffn2"),
        mix_pre=w["mix_pre_g"].reshape(1, d), mix_post=w["mix_post_g"].reshape(1, d),
        w1=cols("gq", "gk", "gv", "fq", "fk", "fv"), ws=ws, wa2p=wa2p,
        b_a=w["gla_b_a"].reshape(1, gk_w), b_f=b_f,
        w2=cols("gr", "mq", "gates"),
        gn=w["gla_norm_g"].reshape(1, d),
        wgo=w["w_gla_o"].astype(BF16), wfo=w["w_fox_o"].astype(BF16), wmo=w["w_mem_o"].astype(BF16),
        wout=w["w_out"].astype(BF16),
        mem_g=w["mem_norm_g"].reshape(1, d), w_mem_kv=w["w_mem_kv"].astype(BF16),
    )


def _ffn_apply(x2, p):
    return _ffn(x2, p["pre"], p["post"], p["wg"], p["wu"], p["wd"])


def _forget_layouts(f, lead=None):
    b, t, h = f.shape
    if lead is not None:
        f = jnp.concatenate([lead.astype(F32), f], axis=1)
    total = f.shape[1]
    padded = -(-total // LANES) * LANES
    f = jnp.pad(f, ((0, 0), (0, padded - total), (0, 0)))
    rows = _cumsum_time(jnp.swapaxes(f, 1, 2).reshape(b * h, padded // LANES, LANES))
    rows = rows.reshape(b, h, padded)
    cols = jnp.swapaxes(rows[:, :, total - t:total], 1, 2)
    return cols, rows


def _group_layer(x2, b, t, p, layer, depth, carried, gla_state, fox, memory, augment):
    d = x2.shape[1]
    x2 = _ffn_apply(x2, p["ffn1"])
    gq, gk, gv, la, fq, fk16, fv16, *carried = _mix_in(x2, p["mix_pre"], p["w1"], p["ws"], p["wa2p"], p["b_a"],
                                                       p["b_f"], layer, depth, carried,
                                                       seq=t if augment else None)
    r3 = lambda a: a.reshape(b, t, a.shape[1])
    o_gla, s_fin = _gla(r3(gq), r3(gk), r3(gv), r3(la), gla_state[0], gla_state[1])
    f_log = carried[2][layer].reshape(b, t, SMALL_W)[:, :, :FOX_HEADS]
    o_fox = fox(r3(fq), r3(fk16), r3(fv16), f_log)
    x2 = _mix_out(x2, o_gla.reshape(b * t, d), o_fox.reshape(b * t, d), memory[0], memory[1], memory[2], t,
                  p["mix_pre"], p["w2"], p["gn"], p["wgo"], p["wfo"], p["wmo"], p["wout"], p["mix_post"])
    x2 = _ffn_apply(x2, p["ffn2"])
    return x2, carried, s_fin


def kernel(x_prompt, x_sample, cache_fox_k, cache_fox_v, cache_fox_logf, state_gla, cache_mem_k, cache_mem_v, mem_prompt, ffn1_pre_g, ffn1_post_g, ffn1_w_gate, ffn1_w_up, ffn1_w_down, mix_pre_g, mix_post_g, w_in, gla_w_a2, gla_b_a, fox_b_f, gla_norm_g, w_gla_o, w_fox_o, w_mem_o, w_out, mem_norm_g, w_mem_kv, ffn2_pre_g, ffn2_post_g, ffn2_w_gate, ffn2_w_up, ffn2_w_down):
    weights = dict(ffn1_pre_g=ffn1_pre_g, ffn1_post_g=ffn1_post_g, ffn1_w_gate=ffn1_w_gate, ffn1_w_up=ffn1_w_up,
                   ffn1_w_down=ffn1_w_down, mix_pre_g=mix_pre_g, mix_post_g=mix_post_g, w_in=w_in,
                   gla_w_a2=gla_w_a2, gla_b_a=gla_b_a, fox_b_f=fox_b_f, gla_norm_g=gla_norm_g, w_gla_o=w_gla_o,
                   w_fox_o=w_fox_o, w_mem_o=w_mem_o, w_out=w_out, mem_norm_g=mem_norm_g, w_mem_kv=w_mem_kv,
                   ffn2_pre_g=ffn2_pre_g, ffn2_post_g=ffn2_post_g, ffn2_w_gate=ffn2_w_gate, ffn2_w_up=ffn2_w_up,
                   ffn2_w_down=ffn2_w_down)
    depth = w_in.shape[0]
    bp, tp, d = x_prompt.shape
    bs, ts, _ = x_sample.shape
    hd = d // FOX_HEADS
    dk, dv = state_gla.shape[-2:]
    layers = [_prep_layer(d, {k: v[l] for k, v in weights.items()}) for l in range(depth)]

    x = x_prompt.reshape(bp * tp, d)
    zero_state = jnp.zeros((bp, GLA_HEADS, dk, dv), F32)
    tq = min(512, tp)

    def fox_p(q_aug, k_aug, fv, f_log):
        return _fox_prompt(q_aug, k_aug, fv, tq=tq)

    carried, p_state, p_mk, p_mv = [], [], [], []
    for l in range(depth):
        mk, mv, mk16, mv16 = _mem_kv(mem_prompt, layers[l]["mem_g"], layers[l]["w_mem_kv"])
        x, carried, s_fin = _group_layer(x, bp, tp, layers[l], l, depth, carried, (zero_state, 0), fox_p,
                                         (mk16, mv16, 0), True)
        p_state.append(s_fin)
        p_mk.append(mk)
        p_mv.append(mv)
    y_prompt = x.reshape(bp, tp, d)
    p_fox_k = carried[0].reshape(depth, bp, tp, FOX_HEADS, hd)
    p_fox_v = carried[1].reshape(depth, bp, tp, FOX_HEADS, hd)
    p_fox_logf = carried[2].reshape(depth, bp, tp, SMALL_W)[..., :FOX_HEADS]
    p_mem_k = jnp.stack(p_mk)
    p_mem_v = jnp.stack(p_mv)

    x = x_sample.reshape(bs * ts, d)
    past = cache_fox_k.shape[2]
    ck = cache_fox_k.reshape(depth * bs, past, FOX_HEADS, hd)
    cv = cache_fox_v.reshape(depth * bs, past, FOX_HEADS, hd)
    states = state_gla.reshape(depth * bs, GLA_HEADS, dk, dv)
    mem_s = cache_mem_k.shape[2]
    cmk = cache_mem_k.reshape(depth * bs, mem_s, d).astype(BF16)
    cmv = cache_mem_v.reshape(depth * bs, mem_s, d).astype(BF16)
    carried, s_state = [], []
    for l in range(depth):
        def fox_s(fq, fk, fv, f_log, l=l):
            _, f_rows = _forget_layouts(f_log, lead=cache_fox_logf[l])
            f_all = jnp.swapaxes(f_rows[:, :, :past + ts], 1, 2)
            return _fox_sample(fq, fk, fv, ck, cv, l * bs, f_all)

        x, carried, s_new = _group_layer(x, bs, ts, layers[l], l, depth, carried, (states, l * bs), fox_s,
                                         (cmk, cmv, l * bs), False)
        s_state.append(s_new)
    y_sample = x.reshape(bs, ts, d)
    s_fox_k = carried[0].reshape(depth, bs, ts, FOX_HEADS, hd)
    s_fox_v = carried[1].reshape(depth, bs, ts, FOX_HEADS, hd)
    s_fox_logf = carried[2].reshape(depth, bs, ts, SMALL_W)[..., :FOX_HEADS]

    return (y_prompt, y_sample, p_fox_k, p_fox_v, p_fox_logf, jnp.stack(p_state), p_mem_k, p_mem_v,
            s_fox_k, s_fox_v, s_fox_logf, jnp.stack(s_state))
```

```python
import functools

import numpy as np
import jax
import jax.numpy as jnp
from jax import lax
from jax.experimental import pallas as pl
from jax.experimental.pallas import tpu as pltpu

F32 = jnp.float32
BF16 = jnp.bfloat16

EPS = 1e-6
GLA_HEADS = 4
GLA_RANK = 16
GLA_TAU = 16.0
FOX_HEADS = 8
MEM_HEADS = 4
N_BRANCH = 3

LANES = 128
SUBLANES = 8
V7X_VMEM_LIMIT = 56 * 1024 * 1024
FFN_CHUNK = 256
GLA_CHUNK = 64
GLA_SUB = 16
FOX_HEADS_PER_STEP = 4
FOX_WIDE = 2
LOG2E = 1.4426950408889634
SMALL_W = 128


def _dot(a, b):
    return jnp.dot(a, b, preferred_element_type=F32)


def _dot_nt(a, b):
    return lax.dot_general(a, b, (((1,), (1,)), ((), ())), preferred_element_type=F32)


def _dot_tn(a, b):
    return lax.dot_general(a, b, (((0,), (0,)), ((), ())), preferred_element_type=F32)


def _rmsnorm(x, g):
    return x * lax.rsqrt(jnp.mean(x * x, axis=-1, keepdims=True) + EPS) * g


def _sigmoid(x):
    return 1.0 / (1.0 + jnp.exp(-x))


def _log_sigmoid(x):
    return jnp.minimum(x, 0.0) - jnp.log1p(jnp.exp(-jnp.abs(x)))


def _split3(x):
    hi = x.astype(BF16)
    r1 = x - hi.astype(F32)
    mid = r1.astype(BF16)
    lo = (r1 - mid.astype(F32)).astype(BF16)
    return hi, mid, lo


def _resident(shape):
    nd = len(shape)
    return pl.BlockSpec(shape, lambda *_: (0,) * nd, pipeline_mode=pl.Buffered(1))


def _params(semantics):
    return pltpu.CompilerParams(dimension_semantics=semantics, vmem_limit_bytes=V7X_VMEM_LIMIT)


def _row_tile(m, want):
    t = min(m, want)
    assert m % t == 0, (m, t)
    return t


def _ffn_body(x_ref, pre_ref, post_ref, wg_ref, wu_ref, wd_ref, o_ref, u_ref, acc_ref):
    x = x_ref[...]
    u_ref[...] = _rmsnorm(x, pre_ref[...]).astype(BF16)
    for c in range(wg_ref.shape[1] // FFN_CHUNK):
        cols = slice(c * FFN_CHUNK, (c + 1) * FFN_CHUNK)
        u = u_ref[...]
        g = _dot(u, wg_ref[:, cols])
        up = _dot(u, wu_ref[:, cols])
        h = (g * _sigmoid(g) * up).astype(BF16)
        part = _dot(h, wd_ref[cols, :])
        if c == 0:
            acc_ref[...] = part
        else:
            acc_ref[...] += part
    o_ref[...] = x + 0.5 * _rmsnorm(acc_ref[...], post_ref[...])


def _ffn(x2, pre_g, post_g, wg, wu, wd, tm=512):
    m, d = x2.shape
    tm = _row_tile(m, tm)
    return pl.pallas_call(
        _ffn_body,
        grid=(m // tm,),
        in_specs=[pl.BlockSpec((tm, d), lambda i: (i, 0)),
                  _resident(pre_g.shape), _resident(post_g.shape),
                  _resident(wg.shape), _resident(wu.shape), _resident(wd.shape)],
        out_specs=pl.BlockSpec((tm, d), lambda i: (i, 0)),
        out_shape=jax.ShapeDtypeStruct((m, d), F32),
        scratch_shapes=[pltpu.VMEM((tm, d), BF16), pltpu.VMEM((tm, d), F32)],
        compiler_params=_params(("parallel",)),
        name="ffn",
    )(x2, pre_g, post_g, wg, wu, wd)


def _mix_in_body(dims, n_carried, tiles_per_seq, x_ref, pre_ref, w_ref, ws_ref, wa2_ref, ba_ref, bf_ref, *refs):
    augment = tiles_per_seq is not None
    if augment:
        place_q_ref, place_k_ref, ones_q_ref, ones_k_ref = refs[:4]
        refs = refs[4:]
    refs = refs[n_carried:]
    gq_ref, gk_ref, gv_ref, la_ref, fq_ref, fk16_ref, fv16_ref, fk32_ref, fv32_ref, fl_ref, u_ref = refs[:11]
    gk_w, gv_w, fx_w, dk_scale, hd_scale = dims
    u_ref[...] = _rmsnorm(x_ref[...], pre_ref[...]).astype(BF16)

    def proj(lo, width):
        return _dot(u_ref[...], w_ref[:, lo:lo + width])

    off = 0
    gq_ref[...] = proj(off, gk_w) * dk_scale
    off += gk_w
    gk_ref[...] = proj(off, gk_w)
    off += gk_w
    gv_ref[...] = proj(off, gv_w).astype(BF16)
    off += gv_w
    fq16 = (proj(off, fx_w) * hd_scale).astype(BF16)
    off += fx_w
    hd = fx_w // FOX_HEADS
    fk = proj(off, fx_w)
    fk16 = fk.astype(BF16)
    off += fx_w
    fv = proj(off, fx_w)
    fv16_ref[...] = fv.astype(BF16)
    for hh in range(FOX_HEADS):
        fk32_ref[:, hh, :] = fk[:, hh * hd:(hh + 1) * hd]
        fv32_ref[:, hh, :] = fv[:, hh * hd:(hh + 1) * hd]

    small = _dot(u_ref[...], ws_ref[...])
    lane = lax.broadcasted_iota(jnp.int32, small.shape, 1)
    log_f = _log_sigmoid(small + bf_ref[...])
    fl_ref[...] = jnp.where(lane < FOX_HEADS, log_f, 0.0)
    lr = jnp.where((lane >= FOX_HEADS) & (lane < FOX_HEADS + GLA_RANK), small, 0.0).astype(BF16)
    la_ref[...] = _log_sigmoid(_dot(lr, wa2_ref[...]) + ba_ref[...]) * (1.0 / GLA_TAU)

    if not augment:
        fq_ref[...] = fq16
        fk16_ref[...] = fk16
        return

    f_ref, carry_ref = refs[11:]
    tm = small.shape[0]
    copies = FOX_HEADS + GLA_RANK
    is_f = (lane < FOX_HEADS) | ((lane >= copies) & (lane < copies + 2 * FOX_HEADS))
    src = jnp.where(is_f, log_f, 0.0)

    @pl.when(pl.program_id(0) % tiles_per_seq == 0)
    def _():
        carry_ref[...] = jnp.zeros_like(carry_ref)

    r = lax.broadcasted_iota(jnp.int32, (LANES, LANES), 0)
    c = lax.broadcasted_iota(jnp.int32, (LANES, LANES), 1)
    lower = jnp.where(r >= c, 1.0, 0.0).astype(BF16)
    for j in range(tm // LANES):
        rows = slice(j * LANES, (j + 1) * LANES)
        hi, mid, lo = _split3(src[rows])
        f_ref[rows, :] = _dot(lower, hi) + _dot(lower, mid) + _dot(lower, lo)
        f_ref[rows, :] = f_ref[rows, :] + carry_ref[...]
        carry_ref[...] = f_ref[(j + 1) * LANES - 1:(j + 1) * LANES, :]
    hi, mid, lo = _split3(f_ref[...] * LOG2E)
    pieces = jnp.where(lane < FOX_HEADS, hi, jnp.where(lane < copies + FOX_HEADS, mid, lo))
    extra_q = (_dot(pieces, place_q_ref[...]) + ones_q_ref[...]).astype(BF16)
    extra_k = (_dot(pieces, place_k_ref[...]) + ones_k_ref[...]).astype(BF16)
    q_parts, k_parts = [], []
    for hh in range(FOX_HEADS):
        sl = slice(hh * hd, (hh + 1) * hd)
        q_parts += [fq16[:, sl], extra_q[:, sl]]
        k_parts += [fk16[:, sl], extra_k[:, sl]]
    fq_ref[...] = jnp.concatenate(q_parts, axis=1)
    fk16_ref[...] = jnp.concatenate(k_parts, axis=1)


def _augment_constants(d):
    hd = d // FOX_HEADS
    copies = FOX_HEADS + GLA_RANK
    place_q = np.zeros((SMALL_W, d), np.float32)
    place_k = np.zeros((SMALL_W, d), np.float32)
    ones_q = np.zeros((1, d), np.float32)
    ones_k = np.zeros((1, d), np.float32)
    for hh in range(FOX_HEADS):
        for j, lane in enumerate((hh, copies + hh, copies + FOX_HEADS + hh)):
            place_q[lane, hh * hd + j] = 1.0
            place_k[lane, hh * hd + 3 + j] = -1.0
            ones_q[0, hh * hd + 3 + j] = 1.0
            ones_k[0, hh * hd + j] = 1.0
    return jnp.asarray(place_q, BF16), jnp.asarray(place_k, BF16), jnp.asarray(ones_q), jnp.asarray(ones_k)


def _mix_in(x2, pre_g, w1, ws, wa2p, b_a, b_f, layer, depth, carried, seq=None, tm=512):
    m, d = x2.shape
    tm = _row_tile(m, tm)
    gk_w = d // 2
    dims = (gk_w, d, d, float((gk_w // GLA_HEADS) ** -0.5), float((d // FOX_HEADS) ** -0.5) * LOG2E)
    augment = seq is not None
    assert not augment or (seq % tm == 0 and tm % LANES == 0)

    def rows(width):
        return pl.BlockSpec((tm, width), lambda i: (i, 0))

    def stacked(*tail):
        return pl.BlockSpec((None, tm) + tail, lambda i: (layer, i) + (0,) * len(tail))

    qk_w = 2 * d if augment else d
    outs = [(gk_w, F32), (gk_w, F32), (d, BF16), (gk_w, F32), (qk_w, BF16), (qk_w, BF16), (d, BF16)]
    stacks = [(FOX_HEADS, d // FOX_HEADS), (FOX_HEADS, d // FOX_HEADS), (SMALL_W,)]
    consts = _augment_constants(d) if augment else ()
    n_in = 7 + len(consts)
    scratch = [pltpu.VMEM((tm, d), BF16)]
    if augment:
        scratch += [pltpu.VMEM((tm, SMALL_W), F32), pltpu.VMEM((1, SMALL_W), F32)]
    return pl.pallas_call(
        functools.partial(_mix_in_body, dims, len(carried), seq // tm if augment else None),
        grid=(m // tm,),
        in_specs=[rows(d), _resident(pre_g.shape), _resident(w1.shape), _resident(ws.shape),
                  _resident(wa2p.shape), _resident(b_a.shape), _resident(b_f.shape)]
                 + [_resident(c.shape) for c in consts]
                 + [pl.BlockSpec(memory_space=pl.ANY)] * len(carried),
        out_specs=[rows(w) for w, _ in outs] + [stacked(*tail) for tail in stacks],
        out_shape=[jax.ShapeDtypeStruct((m, w), dt) for w, dt in outs]
                  + [jax.ShapeDtypeStruct((depth, m) + tail, F32) for tail in stacks],
        input_output_aliases={n_in + j: len(outs) + j for j in range(len(carried))},
        scratch_shapes=scratch,
        compiler_params=_params(("arbitrary",) if augment else ("parallel",)),
        name="mix_in",
    )(x2, pre_g, w1, ws, wa2p, b_a, b_f, *consts, *carried)


def _cumsum_body(x_ref, m_ref, o_ref):
    x = x_ref[...]
    r = lax.broadcasted_iota(jnp.int32, (LANES, LANES), 0)
    c = lax.broadcasted_iota(jnp.int32, (LANES, LANES), 1)
    upper = jnp.where(r <= c, 1.0, 0.0).astype(BF16)
    hi, mid, lo = _split3(x)
    o_ref[...] = _dot(hi, upper) + _dot(mid, upper) + _dot(lo, upper)
    tot = jnp.broadcast_to(o_ref[:, LANES - 1:LANES], x.shape)
    hi, mid, lo = _split3(tot)
    mm = m_ref[...]
    offset = _dot(mm, hi) + _dot(mm, mid) + _dot(mm, lo)
    o_ref[...] = o_ref[...] + offset


def _cumsum_time(x3):
    g, n, _ = x3.shape
    per = 8
    while (per * n) % 8:
        per += 1
    gpb = per if g % per == 0 else g
    rb = gpb * n
    idx = np.arange(rb)
    earlier = (idx[:, None] // n == idx[None, :] // n) & (idx[None, :] < idx[:, None])
    mm = jnp.asarray(earlier.astype(np.float32), BF16)
    out = pl.pallas_call(
        _cumsum_body,
        grid=(g // gpb,),
        in_specs=[pl.BlockSpec((rb, LANES), lambda i: (i, 0)), _resident((rb, rb))],
        out_specs=pl.BlockSpec((rb, LANES), lambda i: (i, 0)),
        out_shape=jax.ShapeDtypeStruct((g * n, LANES), F32),
        compiler_params=_params(("parallel",)),
        name="cumsum",
    )(x3.reshape(g * n, LANES), mm)
    return out.reshape(g, n, LANES)


def _gla_chunk(q, k, v, la, st):
    cl, dk = q.shape
    sub = min(GLA_SUB, cl)
    row = lax.broadcasted_iota(jnp.int32, (cl, cl), 0)
    col = lax.broadcasted_iota(jnp.int32, (cl, cl), 1)
    lower = jnp.where(row >= col, 1.0, 0.0).astype(BF16)
    hi, mid, lo = _split3(la)
    b = _dot(lower, hi) + _dot(lower, mid) + _dot(lower, lo)
    b_end = b[cl - 1:cl, :]

    o = _dot_nt((q * jnp.exp(b)).astype(BF16), st.astype(BF16))

    tsub = lax.broadcasted_iota(jnp.int32, (SUBLANES, dk), 0)
    acol = lax.broadcasted_iota(jnp.int32, (SUBLANES, cl), 1)
    blocks = []
    for i in range(cl // sub):
        r0 = i * sub
        b_i, q_i, k_i = b[r0:r0 + sub], q[r0:r0 + sub], k[r0:r0 + sub]
        groups = [jnp.zeros((SUBLANES, cl), F32) for _ in range(sub // SUBLANES)]
        for s in range(sub):
            own = s // SUBLANES
            for gi in range(own, sub // SUBLANES):
                rows = slice(gi * SUBLANES, (gi + 1) * SUBLANES)
                rel = b_i[rows] - b_i[s:s + 1]
                if gi == own:
                    rel = jnp.where(tsub >= s - gi * SUBLANES, rel, -jnp.inf)
                w = jnp.sum(q_i[rows] * k_i[s:s + 1] * jnp.exp(rel), axis=1, keepdims=True)
                groups[gi] = jnp.where(acol == r0 + s, w, groups[gi])
        a_i = jnp.concatenate(groups, axis=0)
        if i > 0:
            ref = b[r0 - 1:r0]
            q_t = (q_i * jnp.exp(b_i - ref)).astype(BF16)
            k_t = jnp.concatenate([k[:r0] * jnp.exp(ref - b[:r0]), jnp.zeros((cl - r0, dk), F32)], axis=0)
            a_i = a_i + _dot_nt(q_t, k_t.astype(BF16))
        blocks.append(a_i)
    attn = jnp.concatenate(blocks, axis=0).astype(BF16)
    o = o + _dot(attn, v)

    k_d = (k * jnp.exp(b_end - b)).astype(BF16)
    st_new = st * jnp.exp(b_end) + _dot_tn(v, k_d)
    return o, st_new


def _gla_body(cl, q_ref, k_ref, v_ref, la_ref, s0_ref, o_ref, sf_ref, st_ref):
    t = pl.program_id(1)
    heads, dv, dk = st_ref.shape

    @pl.when(t == 0)
    def _():
        for hh in range(heads):
            st_ref[hh] = s0_ref[0, hh].T

    n_chunks = q_ref.shape[1] // cl

    def chunk(ci, carry):
        r = pl.multiple_of(ci * cl, cl)
        for hh in range(heads):
            ks = slice(hh * dk, (hh + 1) * dk)
            vs = slice(hh * dv, (hh + 1) * dv)
            o, st = _gla_chunk(q_ref[0, pl.ds(r, cl), ks], k_ref[0, pl.ds(r, cl), ks], v_ref[0, pl.ds(r, cl), vs],
                               la_ref[0, pl.ds(r, cl), ks], st_ref[hh])
            o_ref[0, pl.ds(r, cl), vs] = o.astype(o_ref.dtype)
            st_ref[hh] = st
        return carry

    lax.fori_loop(0, n_chunks, chunk, 0, unroll=2 if n_chunks % 2 == 0 else 1)

    @pl.when(t == pl.num_programs(1) - 1)
    def _():
        for hh in range(heads):
            sf_ref[0, hh] = st_ref[hh].T


def _gla(gq, gk, gv, la, s0, s0_off=0, tt=512):
    b, t, _ = gq.shape
    _, h, dk, dv = s0.shape
    cl = min(GLA_CHUNK, t)
    tt = _row_tile(t, tt)
    return pl.pallas_call(
        functools.partial(_gla_body, cl),
        grid=(b, t // tt),
        in_specs=[pl.BlockSpec((1, tt, h * dk), lambda i, n: (i, n, 0)),
                  pl.BlockSpec((1, tt, h * dk), lambda i, n: (i, n, 0)),
                  pl.BlockSpec((1, tt, h * dv), lambda i, n: (i, n, 0)),
                  pl.BlockSpec((1, tt, h * dk), lambda i, n: (i, n, 0)),
                  pl.BlockSpec((1, h, dk, dv), lambda i, n: (i + s0_off, 0, 0, 0))],
        out_specs=[pl.BlockSpec((1, tt, h * dv), lambda i, n: (i, n, 0)),
                   pl.BlockSpec((1, h, dk, dv), lambda i, n: (i, 0, 0, 0))],
        out_shape=[jax.ShapeDtypeStruct((b, t, h * dv), BF16), jax.ShapeDtypeStruct((b, h, dk, dv), F32)],
        scratch_shapes=[pltpu.VMEM((h, dv, dk), F32)],
        compiler_params=_params(("parallel", "arbitrary")),
        name="gla",
    )(gq, gk, gv, la, s0)


def _fox_prompt_body(hps, q_ref, k_ref, v_ref, o_ref, m_ref, acc_ref):
    qi = pl.program_id(2)
    tq = q_ref.shape[1]
    hw = q_ref.shape[2] // hps
    hd = v_ref.shape[2] // hps
    wide = FOX_WIDE * tq

    def logits(hh, r, width):
        sl = slice(hh * hw, (hh + 1) * hw)
        return _dot_nt(q_ref[0, :, sl], k_ref[0, pl.ds(r, width), sl])

    def values(hh, r, width):
        lane = lax.broadcasted_iota(jnp.int32, (width, LANES), 1)
        ones_col = jnp.where(lane == 0, 1.0, 0.0).astype(BF16)
        return jnp.concatenate([v_ref[0, pl.ds(r, width), hh * hd:(hh + 1) * hd], ones_col], axis=1)

    def first(r, width):
        row = lax.broadcasted_iota(jnp.int32, (tq, width), 0)
        col = lax.broadcasted_iota(jnp.int32, (tq, width), 1)
        for hh in range(hps):
            z = jnp.where(col <= row + (width - tq), logits(hh, r, width), -jnp.inf)
            m = jnp.max(z, axis=1, keepdims=True)
            m_ref[hh] = m
            acc_ref[hh] = _dot(jnp.exp2(z - m).astype(BF16), values(hh, r, width))

    assert FOX_WIDE == 2
    lead = jnp.where(qi % 2 == 1, 1, jnp.where(qi >= 2, 2, 0))

    for extra_blocks in range(3):
        @pl.when(lead == extra_blocks)
        def _(extra_blocks=extra_blocks):
            first(pl.multiple_of((qi - extra_blocks) * tq, tq), (extra_blocks + 1) * tq)

    def update(r, width):
        for hh in range(hps):
            z = logits(hh, r, width)
            m_old = m_ref[hh]
            m_new = jnp.maximum(m_old, jnp.max(z, axis=1, keepdims=True))
            m_ref[hh] = m_new
            pv = _dot(jnp.exp2(z - m_new).astype(BF16), values(hh, r, width))
            acc_ref[hh] = jnp.exp2(m_old - m_new) * acc_ref[hh] + pv

    n_wide = (qi - lead) // FOX_WIDE

    @pl.when(n_wide % 2 == 1)
    def _():
        update(0, wide)

    base = (n_wide % 2) * wide

    def double_wide_block(j, carry):
        update(pl.multiple_of(base + j * 2 * wide, wide), 2 * wide)
        return carry

    lax.fori_loop(0, n_wide // 2, double_wide_block, 0)
    for hh in range(hps):
        acc = acc_ref[hh]
        o_ref[0, :, hh * hd:(hh + 1) * hd] = (acc[:, :hd] / acc[:, hd:hd + 1]).astype(o_ref.dtype)


def _fox_prompt(q_aug, k_aug, fv, tq=512, hps=FOX_HEADS_PER_STEP):
    b, t, d = fv.shape
    h = FOX_HEADS
    hd = d // h
    nq = t // tq
    keys = pl.BlockSpec((1, t, hps * 2 * hd), lambda i, j, n: (i, 0, j), pipeline_mode=pl.Buffered(1))
    vals = pl.BlockSpec((1, t, hps * hd), lambda i, j, n: (i, 0, j))
    return pl.pallas_call(
        functools.partial(_fox_prompt_body, hps),
        grid=(b, h // hps, nq),
        in_specs=[pl.BlockSpec((1, tq, hps * 2 * hd), lambda i, j, n: (i, n, j)), keys, vals],
        out_specs=pl.BlockSpec((1, tq, hps * hd), lambda i, j, n: (i, n, j)),
        out_shape=jax.ShapeDtypeStruct((b, t, d), BF16),
        scratch_shapes=[pltpu.VMEM((hps, tq, 1), F32), pltpu.VMEM((hps, tq, hd + LANES), F32)],
        compiler_params=_params(("parallel", "parallel", "arbitrary")),
        name="fox_prompt",
    )(q_aug, k_aug, fv)


def _fox_sample_body(heads, q_ref, kc_ref, vc_ref, kn_ref, vn_ref, fq_ref, fkc_ref, fkn_ref, o_ref,
                     m_ref, l_ref, acc_ref):
    c = pl.program_id(1)
    q = q_ref[0]
    rows = q.shape[0]
    ln = rows // heads
    f_q = fq_ref[0] * LOG2E
    head_q = lax.broadcasted_iota(jnp.int32, (rows, 1), 0) // ln

    def head_k(n):
        return lax.broadcasted_iota(jnp.int32, (1, n), 1) % heads

    @pl.when(c == 0)
    def _():
        n = kn_ref.shape[1]
        z = _dot_nt(q, kn_ref[0]) + f_q - fkn_ref[0] * LOG2E
        step_k = lax.broadcasted_iota(jnp.int32, (1, n), 1) // heads
        step_q = lax.broadcasted_iota(jnp.int32, (rows, 1), 0) % ln
        z = jnp.where((head_k(n) == head_q) & (step_k <= step_q), z, -jnp.inf)
        m = jnp.max(z, axis=1, keepdims=True)
        p = jnp.exp2(z - m)
        m_ref[...] = m
        l_ref[...] = jnp.sum(p, axis=1, keepdims=True)
        acc_ref[...] = _dot(p.astype(BF16), vn_ref[0])

    n = kc_ref.shape[1]
    z = _dot_nt(q, kc_ref[0].astype(BF16)) + f_q - fkc_ref[0] * LOG2E
    z = jnp.where(head_k(n) == head_q, z, -jnp.inf)
    m_old = m_ref[...]
    m_new = jnp.maximum(m_old, jnp.max(z, axis=1, keepdims=True))
    alpha = jnp.exp2(m_old - m_new)
    p = jnp.exp2(z - m_new)
    m_ref[...] = m_new
    l_ref[...] = alpha * l_ref[...] + jnp.sum(p, axis=1, keepdims=True)
    acc_ref[...] = alpha * acc_ref[...] + _dot(p.astype(BF16), vc_ref[0].astype(BF16))

    @pl.when(c == pl.num_programs(1) - 1)
    def _():
        o_ref[0] = (acc_ref[...] / l_ref[...]).astype(o_ref.dtype)


def _fox_sample(fq, fk, fv, cache_k, cache_v, cache_off, f_all, chunk_steps=1024):
    b, ln, d = fq.shape
    rows_total, past, h, hd = cache_k.shape
    assert h & (h - 1) == 0 and ln & (ln - 1) == 0
    chunk_steps = _row_tile(past, chunk_steps)
    ch = chunk_steps * h
    q_rows = fq.reshape(b, ln, h, hd).transpose(0, 2, 1, 3).reshape(b, h * ln, hd)
    f_q = jnp.swapaxes(f_all[:, past:, :], 1, 2).reshape(b, h * ln, 1)
    f_k = f_all.reshape(b, 1, (past + ln) * h)
    new = pl.BlockSpec((1, ln * h, hd), lambda i, c: (i, 0, 0))
    cache = pl.BlockSpec((1, ch, hd), lambda i, c: (i + cache_off, c, 0))
    o = pl.pallas_call(
        functools.partial(_fox_sample_body, h),
        grid=(b, past // chunk_steps),
        in_specs=[new, cache, cache, new, new,
                  pl.BlockSpec((1, h * ln, 1), lambda i, c: (i, 0, 0)),
                  pl.BlockSpec((1, 1, ch), lambda i, c: (i, 0, c)),
                  pl.BlockSpec((1, 1, ln * h), lambda i, c: (i, 0, 0))],
        out_specs=new,
        out_shape=jax.ShapeDtypeStruct((b, h * ln, hd), BF16),
        scratch_shapes=[pltpu.VMEM((h * ln, 1), F32), pltpu.VMEM((h * ln, 1), F32), pltpu.VMEM((h * ln, hd), F32)],
        compiler_params=_params(("parallel", "arbitrary")),
        name="fox_sample",
    )(q_rows, cache_k.reshape(rows_total, past * h, hd), cache_v.reshape(rows_total, past * h, hd),
      fk.reshape(b, ln * h, hd), fv.reshape(b, ln * h, hd), f_q, f_k[:, :, :past * h], f_k[:, :, past * h:])
    return o.reshape(b, h, ln, hd).transpose(0, 2, 1, 3).reshape(b, ln, d)


def _mem_kv_body(mem_ref, g_ref, w_ref, k_ref, v_ref, k16_ref, v16_ref):
    u = _rmsnorm(mem_ref[0], g_ref[...]).astype(BF16)
    _, _, heads, hd = k_ref.shape
    d = heads * hd
    k = _dot(u, w_ref[:, :d])
    v = _dot(u, w_ref[:, d:])
    k16_ref[0] = k.astype(BF16)
    v16_ref[0] = v.astype(BF16)
    for hh in range(heads):
        k_ref[0, :, hh, :] = k[:, hh * hd:(hh + 1) * hd]
        v_ref[0, :, hh, :] = v[:, hh * hd:(hh + 1) * hd]


def _mem_kv(mem, g, w):
    b, n, d = mem.shape
    hd = d // MEM_HEADS
    flat = pl.BlockSpec((1, n, d), lambda i: (i, 0, 0))
    out = pl.BlockSpec((1, n, MEM_HEADS, hd), lambda i: (i, 0, 0, 0))
    return pl.pallas_call(
        _mem_kv_body,
        grid=(b,),
        in_specs=[flat, _resident(g.shape), _resident(w.shape)],
        out_specs=[out, out, flat, flat],
        out_shape=[jax.ShapeDtypeStruct((b, n, MEM_HEADS, hd), F32)] * 2 + [jax.ShapeDtypeStruct((b, n, d), BF16)] * 2,
        compiler_params=_params(("parallel",)),
        name="mem_kv",
    )(mem, g, w)


def _mix_out_body(seq, x_ref, og_ref, of_ref, mk_ref, mv_ref, pre_ref, w2_ref, gn_ref, wgo_ref, wfo_ref, wmo_ref,
                  wout_ref, post_ref, o_ref, u_ref):
    x = x_ref[...]
    tm, d = x.shape
    u_ref[...] = _rmsnorm(x, pre_ref[...]).astype(BF16)

    def proj(j):
        return _dot(u_ref[...], w2_ref[:, j * d:(j + 1) * d])

    mq = proj(1)
    hd = d // MEM_HEADS
    scale = float(hd ** -0.5)
    per_batch = []
    for j in range(tm // seq):
        heads = []
        for hh in range(MEM_HEADS):
            sl = slice(hh * hd, (hh + 1) * hd)
            s = _dot_nt(mq[j * seq:(j + 1) * seq, sl].astype(BF16), mk_ref[j, :, sl]) * scale
            e = jnp.exp(s - jnp.max(s, axis=1, keepdims=True))
            p = e / jnp.sum(e, axis=1, keepdims=True)
            heads.append(_dot(p.astype(BF16), mv_ref[j, :, sl]))
        per_batch.append(jnp.concatenate(heads, axis=1))
    o_mem = per_batch[0] if len(per_batch) == 1 else jnp.concatenate(per_batch, axis=0)
    y = _sigmoid(proj(4)) * _dot(o_mem.astype(BF16), wmo_ref[...])

    y = y + _sigmoid(proj(3)) * _dot(of_ref[...], wfo_ref[...])

    og = og_ref[...].astype(F32)
    gd = d // GLA_HEADS
    gn = gn_ref[...]
    normed = []
    for hh in range(GLA_HEADS):
        sl = slice(hh * gd, (hh + 1) * gd)
        normed.append(_rmsnorm(og[:, sl], gn[:, sl]))
    r = proj(0)
    b_gla = _dot((jnp.concatenate(normed, axis=1) * (r * _sigmoid(r))).astype(BF16), wgo_ref[...])
    y = y + _sigmoid(proj(2)) * b_gla

    mixed = _dot(y.astype(BF16), wout_ref[...])
    o_ref[...] = x + _rmsnorm(mixed, post_ref[...])


def _mix_out(x2, o_gla, o_fox, mk, mv, mem_off, seq, pre_g, w2, gn, wgo, wfo, wmo, wout, post_g, tm=512):
    m, d = x2.shape
    if seq < tm:
        nb = 4
        while (m // seq) % nb:
            nb //= 2
        tm = seq * nb
    assert m % tm == 0 and (seq % tm == 0 or tm % seq == 0)
    nb = max(1, tm // seq)
    steps_per_batch = max(1, seq // tm)
    mem = mk.shape[1]
    rows = pl.BlockSpec((tm, d), lambda i: (i, 0))
    assert mem_off % nb == 0
    memb = pl.BlockSpec((nb, mem, d), lambda i: (i // steps_per_batch + mem_off // nb, 0, 0))
    return pl.pallas_call(
        functools.partial(_mix_out_body, min(seq, tm)),
        grid=(m // tm,),
        in_specs=[rows, rows, rows, memb, memb, _resident(pre_g.shape), _resident(w2.shape), _resident(gn.shape),
                  _resident(wgo.shape), _resident(wfo.shape), _resident(wmo.shape), _resident(wout.shape),
                  _resident(post_g.shape)],
        out_specs=rows,
        out_shape=jax.ShapeDtypeStruct((m, d), F32),
        scratch_shapes=[pltpu.VMEM((tm, d), BF16)],
        compiler_params=_params(("parallel",)),
        name="mix_out",
    )(x2, o_gla, o_fox, mk, mv, pre_g, w2, gn, wgo, wfo, wmo, wout, post_g)


def _prep_layer(d, w):
    gk_w = d // 2
    sizes = (gk_w, gk_w, d, GLA_RANK, d, d, d, d, FOX_HEADS, d, N_BRANCH * d)
    offs = np.concatenate([[0], np.cumsum(sizes)]).tolist()
    seg = {n: (offs[i], offs[i + 1]) for i, n in enumerate(
        ("gq", "gk", "gv", "glr", "gr", "fq", "fk", "fv", "ff", "mq", "gates"))}
    w_in = w["w_in"]

    def cols(*names):
        return jnp.concatenate([w_in[:, seg[n][0]:seg[n][1]] for n in names], axis=1).astype(BF16)

    ws = jnp.pad(cols("ff", "glr", "ff", "ff"), ((0, 0), (0, SMALL_W - 3 * FOX_HEADS - GLA_RANK)))
    wa2p = jnp.pad(w["gla_w_a2"].astype(BF16), ((FOX_HEADS, SMALL_W - FOX_HEADS - GLA_RANK), (0, 0)))
    b_f = jnp.concatenate([w["fox_b_f"], jnp.zeros((GLA_RANK,), F32), w["fox_b_f"], w["fox_b_f"]])
    b_f = jnp.pad(b_f, (0, SMALL_W - b_f.shape[0])).reshape(1, SMALL_W)

    def ffn(pfx):
        wg, wu, wd = w[pfx + "_w_gate"], w[pfx + "_w_up"], w[pfx + "_w_down"]
        assert wg.shape[1] % FFN_CHUNK == 0
        return dict(pre=w[pfx + "_pre_g"].reshape(1, d), post=w[pfx + "_post_g"].reshape(1, d),
                    wg=wg.astype(BF16), wu=wu.astype(BF16), wd=wd.astype(BF16))

    return dict(
        ffn1=ffn("ffn1"), ffn2=ffn("ffn2"),
        mix_pre=w["mix_pre_g"].reshape(1, d), mix_post=w["mix_post_g"].reshape(1, d),
        w1=cols("gq", "gk", "gv", "fq", "fk", "fv"), ws=ws, wa2p=wa2p,
        b_a=w["gla_b_a"].reshape(1, gk_w), b_f=b_f,
        w2=cols("gr", "mq", "gates"),
        gn=w["gla_norm_g"].reshape(1, d),
        wgo=w["w_gla_o"].astype(BF16), wfo=w["w_fox_o"].astype(BF16), wmo=w["w_mem_o"].astype(BF16),
        wout=w["w_out"].astype(BF16),
        mem_g=w["mem_norm_g"].reshape(1, d), w_mem_kv=w["w_mem_kv"].astype(BF16),
    )


def _ffn_apply(x2, p):
    return _ffn(x2, p["pre"], p["post"], p["wg"], p["wu"], p["wd"])


def _forget_layouts(f, lead=None):
    b, t, h = f.shape
    if lead is not None:
        f = jnp.concatenate([lead.astype(F32), f], axis=1)
    total = f.shape[1]
    padded = -(-total // LANES) * LANES
    f = jnp.pad(f, ((0, 0), (0, padded - total), (0, 0)))
    rows = _cumsum_time(jnp.swapaxes(f, 1, 2).reshape(b * h, padded // LANES, LANES))
    rows = rows.reshape(b, h, padded)
    cols = jnp.swapaxes(rows[:, :, total - t:total], 1, 2)
    return cols, rows


def _group_layer(x2, b, t, p, layer, depth, carried, gla_state, fox, memory, augment):
    d = x2.shape[1]
    x2 = _ffn_apply(x2, p["ffn1"])
    gq, gk, gv, la, fq, fk16, fv16, *carried = _mix_in(x2, p["mix_pre"], p["w1"], p["ws"], p["wa2p"], p["b_a"],
                                                       p["b_f"], layer, depth, carried,
                                                       seq=t if augment else None)
    r3 = lambda a: a.reshape(b, t, a.shape[1])
    o_gla, s_fin = _gla(r3(gq), r3(gk), r3(gv), r3(la), gla_state[0], gla_state[1])
    f_log = carried[2][layer].reshape(b, t, SMALL_W)[:, :, :FOX_HEADS]
    o_fox = fox(r3(fq), r3(fk16), r3(fv16), f_log)
    x2 = _mix_out(x2, o_gla.reshape(b * t, d), o_fox.reshape(b * t, d), memory[0], memory[1], memory[2], t,
                  p["mix_pre"], p["w2"], p["gn"], p["wgo"], p["wfo"], p["wmo"], p["wout"], p["mix_post"])
    x2 = _ffn_apply(x2, p["ffn2"])
    return x2, carried, s_fin


def kernel(x_prompt, x_sample, cache_fox_k, cache_fox_v, cache_fox_logf, state_gla, cache_mem_k, cache_mem_v, mem_prompt, ffn1_pre_g, ffn1_post_g, ffn1_w_gate, ffn1_w_up, ffn1_w_down, mix_pre_g, mix_post_g, w_in, gla_w_a2, gla_b_a, fox_b_f, gla_norm_g, w_gla_o, w_fox_o, w_mem_o, w_out, mem_norm_g, w_mem_kv, ffn2_pre_g, ffn2_post_g, ffn2_w_gate, ffn2_w_up, ffn2_w_down):
    weights = dict(ffn1_pre_g=ffn1_pre_g, ffn1_post_g=ffn1_post_g, ffn1_w_gate=ffn1_w_gate, ffn1_w_up=ffn1_w_up,
                   ffn1_w_down=ffn1_w_down, mix_pre_g=mix_pre_g, mix_post_g=mix_post_g, w_in=w_in,
                   gla_w_a2=gla_w_a2, gla_b_a=gla_b_a, fox_b_f=fox_b_f, gla_norm_g=gla_norm_g, w_gla_o=w_gla_o,
                   w_fox_o=w_fox_o, w_mem_o=w_mem_o, w_out=w_out, mem_norm_g=mem_norm_g, w_mem_kv=w_mem_kv,
                   ffn2_pre_g=ffn2_pre_g, ffn2_post_g=ffn2_post_g, ffn2_w_gate=ffn2_w_gate, ffn2_w_up=ffn2_w_up,
                   ffn2_w_down=ffn2_w_down)
    depth = w_in.shape[0]
    bp, tp, d = x_prompt.shape
    bs, ts, _ = x_sample.shape
    hd = d // FOX_HEADS
    dk, dv = state_gla.shape[-2:]
    layers = [_prep_layer(d, {k: v[l] for k, v in weights.items()}) for l in range(depth)]

    x = x_prompt.reshape(bp * tp, d)
    zero_state = jnp.zeros((bp, GLA_HEADS, dk, dv), F32)
    tq = min(512, tp)

    def fox_p(q_aug, k_aug, fv, f_log):
        return _fox_prompt(q_aug, k_aug, fv, tq=tq)

    carried, p_state, p_mk, p_mv = [], [], [], []
    for l in range(depth):
        mk, mv, mk16, mv16 = _mem_kv(mem_prompt, layers[l]["mem_g"], layers[l]["w_mem_kv"])
        x, carried, s_fin = _group_layer(x, bp, tp, layers[l], l, depth, carried, (zero_state, 0), fox_p,
                                         (mk16, mv16, 0), True)
        p_state.append(s_fin)
        p_mk.append(mk)
        p_mv.append(mv)
    y_prompt = x.reshape(bp, tp, d)
    p_fox_k = carried[0].reshape(depth, bp, tp, FOX_HEADS, hd)
    p_fox_v = carried[1].reshape(depth, bp, tp, FOX_HEADS, hd)
    p_fox_logf = carried[2].reshape(depth, bp, tp, SMALL_W)[..., :FOX_HEADS]
    p_mem_k = jnp.stack(p_mk)
    p_mem_v = jnp.stack(p_mv)

    x = x_sample.reshape(bs * ts, d)
    past = cache_fox_k.shape[2]
    ck = cache_fox_k.reshape(depth * bs, past, FOX_HEADS, hd)
    cv = cache_fox_v.reshape(depth * bs, past, FOX_HEADS, hd)
    states = state_gla.reshape(depth * bs, GLA_HEADS, dk, dv)
    mem_s = cache_mem_k.shape[2]
    cmk = cache_mem_k.reshape(depth * bs, mem_s, d).astype(BF16)
    cmv = cache_mem_v.reshape(depth * bs, mem_s, d).astype(BF16)
    carried, s_state = [], []
    for l in range(depth):
        def fox_s(fq, fk, fv, f_log, l=l):
            _, f_rows = _forget_layouts(f_log, lead=cache_fox_logf[l])
            f_all = jnp.swapaxes(f_rows[:, :, :past + ts], 1, 2)
            return _fox_sample(fq, fk, fv, ck, cv, l * bs, f_all)

        x, carried, s_new = _group_layer(x, bs, ts, layers[l], l, depth, carried, (states, l * bs), fox_s,
                                         (cmk, cmv, l * bs), False)
        s_state.append(s_new)
    y_sample = x.reshape(bs, ts, d)
    s_fox_k = carried[0].reshape(depth, bs, ts, FOX_HEADS, hd)
    s_fox_v = carried[1].reshape(depth, bs, ts, FOX_HEADS, hd)
    s_fox_logf = carried[2].reshape(depth, bs, ts, SMALL_W)[..., :FOX_HEADS]

    return (y_prompt, y_sample, p_fox_k, p_fox_v, p_fox_logf, jnp.stack(p_state), p_mem_k, p_mem_v,
            s_fox_k, s_fox_v, s_fox_logf, jnp.stack(s_state))
```

```python
import functools

import numpy as np
import jax
import jax.numpy as jnp
from jax import lax
from jax.experimental import pallas as pl
from jax.experimental.pallas import tpu as pltpu

F32 = jnp.float32
BF16 = jnp.bfloat16

EPS = 1e-6
GLA_HEADS = 4
GLA_RANK = 16
GLA_TAU = 16.0
FOX_HEADS = 8
MEM_HEADS = 4
N_BRANCH = 3

LANES = 128
SUBLANES = 8
V7X_VMEM_LIMIT = 56 * 1024 * 1024
FFN_CHUNK = 256
GLA_CHUNK = 64
GLA_SUB = 16
FOX_HEADS_PER_STEP = 4
FOX_WIDE = 2
LOG2E = 1.4426950408889634
SMALL_W = 128


def _dot(a, b):
    return jnp.dot(a, b, preferred_element_type=F32)


def _dot_nt(a, b):
    return lax.dot_general(a, b, (((1,), (1,)), ((), ())), preferred_element_type=F32)


def _dot_tn(a, b):
    return lax.dot_general(a, b, (((0,), (0,)), ((), ())), preferred_element_type=F32)


def _rmsnorm(x, g):
    return x * lax.rsqrt(jnp.mean(x * x, axis=-1, keepdims=True) + EPS) * g


def _sigmoid(x):
    return 1.0 / (1.0 + jnp.exp(-x))


def _log_sigmoid(x):
    return jnp.minimum(x, 0.0) - jnp.log1p(jnp.exp(-jnp.abs(x)))


def _split3(x):
    hi = x.astype(BF16)
    r1 = x - hi.astype(F32)
    mid = r1.astype(BF16)
    lo = (r1 - mid.astype(F32)).astype(BF16)
    return hi, mid, lo


def _resident(shape):
    nd = len(shape)
    return pl.BlockSpec(shape, lambda *_: (0,) * nd, pipeline_mode=pl.Buffered(1))


def _params(semantics):
    return pltpu.CompilerParams(dimension_semantics=semantics, vmem_limit_bytes=V7X_VMEM_LIMIT)


def _row_tile(m, want):
    t = min(m, want)
    assert m % t == 0, (m, t)
    return t


def _ffn_body(x_ref, pre_ref, post_ref, wg_ref, wu_ref, wd_ref, o_ref, u_ref, acc_ref):
    x = x_ref[...]
    u_ref[...] = _rmsnorm(x, pre_ref[...]).astype(BF16)
    for c in range(wg_ref.shape[1] // FFN_CHUNK):
        cols = slice(c * FFN_CHUNK, (c + 1) * FFN_CHUNK)
        u = u_ref[...]
        g = _dot(u, wg_ref[:, cols])
        up = _dot(u, wu_ref[:, cols])
        h = (g * _sigmoid(g) * up).astype(BF16)
        part = _dot(h, wd_ref[cols, :])
        if c == 0:
            acc_ref[...] = part
        else:
            acc_ref[...] += part
    o_ref[...] = x + 0.5 * _rmsnorm(acc_ref[...], post_ref[...])


def _ffn(x2, pre_g, post_g, wg, wu, wd, tm=512):
    m, d = x2.shape
    tm = _row_tile(m, tm)
    return pl.pallas_call(
        _ffn_body,
        grid=(m // tm,),
        in_specs=[pl.BlockSpec((tm, d), lambda i: (i, 0)),
                  _resident(pre_g.shape), _resident(post_g.shape),
                  _resident(wg.shape), _resident(wu.shape), _resident(wd.shape)],
        out_specs=pl.BlockSpec((tm, d), lambda i: (i, 0)),
        out_shape=jax.ShapeDtypeStruct((m, d), F32),
        scratch_shapes=[pltpu.VMEM((tm, d), BF16), pltpu.VMEM((tm, d), F32)],
        compiler_params=_params(("parallel",)),
        name="ffn",
    )(x2, pre_g, post_g, wg, wu, wd)


def _mix_in_body(dims, n_carried, tiles_per_seq, x_ref, pre_ref, w_ref, ws_ref, wa2_ref, ba_ref, bf_ref, *refs):
    augment = tiles_per_seq is not None
    if augment:
        place_q_ref, place_k_ref, ones_q_ref, ones_k_ref = refs[:4]
        refs = refs[4:]
    refs = refs[n_carried:]
    gq_ref, gk_ref, gv_ref, la_ref, fq_ref, fk16_ref, fv16_ref, fk32_ref, fv32_ref, fl_ref, u_ref = refs[:11]
    gk_w, gv_w, fx_w, dk_scale, hd_scale = dims
    u_ref[...] = _rmsnorm(x_ref[...], pre_ref[...]).astype(BF16)

    def proj(lo, width):
        return _dot(u_ref[...], w_ref[:, lo:lo + width])

    off = 0
    gq_ref[...] = proj(off, gk_w) * dk_scale
    off += gk_w
    gk_ref[...] = proj(off, gk_w)
    off += gk_w
    gv_ref[...] = proj(off, gv_w).astype(BF16)
    off += gv_w
    fq16 = (proj(off, fx_w) * hd_scale).astype(BF16)
    off += fx_w
    hd = fx_w // FOX_HEADS
    fk = proj(off, fx_w)
    fk16 = fk.astype(BF16)
    off += fx_w
    fv = proj(off, fx_w)
    fv16_ref[...] = fv.astype(BF16)
    for hh in range(FOX_HEADS):
        fk32_ref[:, hh, :] = fk[:, hh * hd:(hh + 1) * hd]
        fv32_ref[:, hh, :] = fv[:, hh * hd:(hh + 1) * hd]

    small = _dot(u_ref[...], ws_ref[...])
    lane = lax.broadcasted_iota(jnp.int32, small.shape, 1)
    log_f = _log_sigmoid(small + bf_ref[...])
    fl_ref[...] = jnp.where(lane < FOX_HEADS, log_f, 0.0)
    lr = jnp.where((lane >= FOX_HEADS) & (lane < FOX_HEADS + GLA_RANK), small, 0.0).astype(BF16)
    la_ref[...] = _log_sigmoid(_dot(lr, wa2_ref[...]) + ba_ref[...]) * (1.0 / GLA_TAU)

    if not augment:
        fq_ref[...] = fq16
        fk16_ref[...] = fk16
        return

    f_ref, carry_ref = refs[11:]
    tm = small.shape[0]
    copies = FOX_HEADS + GLA_RANK
    is_f = (lane < FOX_HEADS) | ((lane >= copies) & (lane < copies + 2 * FOX_HEADS))
    src = jnp.where(is_f, log_f, 0.0)

    @pl.when(pl.program_id(0) % tiles_per_seq == 0)
    def _():
        carry_ref[...] = jnp.zeros_like(carry_ref)

    r = lax.broadcasted_iota(jnp.int32, (LANES, LANES), 0)
    c = lax.broadcasted_iota(jnp.int32, (LANES, LANES), 1)
    lower = jnp.where(r >= c, 1.0, 0.0).astype(BF16)
    for j in range(tm // LANES):
        rows = slice(j * LANES, (j + 1) * LANES)
        hi, mid, lo = _split3(src[rows])
        f_ref[rows, :] = _dot(lower, hi) + _dot(lower, mid) + _dot(lower, lo)
        f_ref[rows, :] = f_ref[rows, :] + carry_ref[...]
        carry_ref[...] = f_ref[(j + 1) * LANES - 1:(j + 1) * LANES, :]
    hi, mid, lo = _split3(f_ref[...] * LOG2E)
    pieces = jnp.where(lane < FOX_HEADS, hi, jnp.where(lane < copies + FOX_HEADS, mid, lo))
    extra_q = (_dot(pieces, place_q_ref[...]) + ones_q_ref[...]).astype(BF16)
    extra_k = (_dot(pieces, place_k_ref[...]) + ones_k_ref[...]).astype(BF16)
    q_parts, k_parts = [], []
    for hh in range(FOX_HEADS):
        sl = slice(hh * hd, (hh + 1) * hd)
        q_parts += [fq16[:, sl], extra_q[:, sl]]
        k_parts += [fk16[:, sl], extra_k[:, sl]]
    fq_ref[...] = jnp.concatenate(q_parts, axis=1)
    fk16_ref[...] = jnp.concatenate(k_parts, axis=1)


def _augment_constants(d):
    hd = d // FOX_HEADS
    copies = FOX_HEADS + GLA_RANK
    place_q = np.zeros((SMALL_W, d), np.float32)
    place_k = np.zeros((SMALL_W, d), np.float32)
    ones_q = np.zeros((1, d), np.float32)
    ones_k = np.zeros((1, d), np.float32)
    for hh in range(FOX_HEADS):
        for j, lane in enumerate((hh, copies + hh, copies + FOX_HEADS + hh)):
            place_q[lane, hh * hd + j] = 1.0
            place_k[lane, hh * hd + 3 + j] = -1.0
            ones_q[0, hh * hd + 3 + j] = 1.0
            ones_k[0, hh * hd + j] = 1.0
    return jnp.asarray(place_q, BF16), jnp.asarray(place_k, BF16), jnp.asarray(ones_q), jnp.asarray(ones_k)


def _mix_in(x2, pre_g, w1, ws, wa2p, b_a, b_f, layer, depth, carried, seq=None, tm=512):
    m, d = x2.shape
    tm = _row_tile(m, tm)
    gk_w = d // 2
    dims = (gk_w, d, d, float((gk_w // GLA_HEADS) ** -0.5), float((d // FOX_HEADS) ** -0.5) * LOG2E)
    augment = seq is not None
    assert not augment or (seq % tm == 0 and tm % LANES == 0)

    def rows(width):
        return pl.BlockSpec((tm, width), lambda i: (i, 0))

    def stacked(*tail):
        return pl.BlockSpec((None, tm) + tail, lambda i: (layer, i) + (0,) * len(tail))

    qk_w = 2 * d if augment else d
    outs = [(gk_w, F32), (gk_w, F32), (d, BF16), (gk_w, F32), (qk_w, BF16), (qk_w, BF16), (d, BF16)]
    stacks = [(FOX_HEADS, d // FOX_HEADS), (FOX_HEADS, d // FOX_HEADS), (SMALL_W,)]
    consts = _augment_constants(d) if augment else ()
    n_in = 7 + len(consts)
    scratch = [pltpu.VMEM((tm, d), BF16)]
    if augment:
        scratch += [pltpu.VMEM((tm, SMALL_W), F32), pltpu.VMEM((1, SMALL_W), F32)]
    return pl.pallas_call(
        functools.partial(_mix_in_body, dims, len(carried), seq // tm if augment else None),
        grid=(m // tm,),
        in_specs=[rows(d), _resident(pre_g.shape), _resident(w1.shape), _resident(ws.shape),
                  _resident(wa2p.shape), _resident(b_a.shape), _resident(b_f.shape)]
                 + [_resident(c.shape) for c in consts]
                 + [pl.BlockSpec(memory_space=pl.ANY)] * len(carried),
        out_specs=[rows(w) for w, _ in outs] + [stacked(*tail) for tail in stacks],
        out_shape=[jax.ShapeDtypeStruct((m, w), dt) for w, dt in outs]
                  + [jax.ShapeDtypeStruct((depth, m) + tail, F32) for tail in stacks],
        input_output_aliases={n_in + j: len(outs) + j for j in range(len(carried))},
        scratch_shapes=scratch,
        compiler_params=_params(("arbitrary",) if augment else ("parallel",)),
        name="mix_in",
    )(x2, pre_g, w1, ws, wa2p, b_a, b_f, *consts, *carried)


def _cumsum_body(x_ref, m_ref, o_ref):
    x = x_ref[...]
    r = lax.broadcasted_iota(jnp.int32, (LANES, LANES), 0)
    c = lax.broadcasted_iota(jnp.int32, (LANES, LANES), 1)
    upper = jnp.where(r <= c, 1.0, 0.0).astype(BF16)
    hi, mid, lo = _split3(x)
    o_ref[...] = _dot(hi, upper) + _dot(mid, upper) + _dot(lo, upper)
    tot = jnp.broadcast_to(o_ref[:, LANES - 1:LANES], x.shape)
    hi, mid, lo = _split3(tot)
    mm = m_ref[...]
    offset = _dot(mm, hi) + _dot(mm, mid) + _dot(mm, lo)
    o_ref[...] = o_ref[...] + offset


def _cumsum_time(x3):
    g, n, _ = x3.shape
    per = 8
    while (per * n) % 8:
        per += 1
    gpb = per if g % per == 0 else g
    rb = gpb * n
    idx = np.arange(rb)
    earlier = (idx[:, None] // n == idx[None, :] // n) & (idx[None, :] < idx[:, None])
    mm = jnp.asarray(earlier.astype(np.float32), BF16)
    out = pl.pallas_call(
        _cumsum_body,
        grid=(g // gpb,),
        in_specs=[pl.BlockSpec((rb, LANES), lambda i: (i, 0)), _resident((rb, rb))],
        out_specs=pl.BlockSpec((rb, LANES), lambda i: (i, 0)),
        out_shape=jax.ShapeDtypeStruct((g * n, LANES), F32),
        compiler_params=_params(("parallel",)),
        name="cumsum",
    )(x3.reshape(g * n, LANES), mm)
    return out.reshape(g, n, LANES)


def _gla_chunk(q, k, v, la, st):
    cl, dk = q.shape
    sub = min(GLA_SUB, cl)
    row = lax.broadcasted_iota(jnp.int32, (cl, cl), 0)
    col = lax.broadcasted_iota(jnp.int32, (cl, cl), 1)
    lower = jnp.where(row >= col, 1.0, 0.0).astype(BF16)
    hi, mid, lo = _split3(la)
    b = _dot(lower, hi) + _dot(lower, mid) + _dot(lower, lo)
    b_end = b[cl - 1:cl, :]

    o = _dot_nt((q * jnp.exp(b)).astype(BF16), st.astype(BF16))

    tsub = lax.broadcasted_iota(jnp.int32, (SUBLANES, dk), 0)
    acol = lax.broadcasted_iota(jnp.int32, (SUBLANES, cl), 1)
    blocks = []
    for i in range(cl // sub):
        r0 = i * sub
        b_i, q_i, k_i = b[r0:r0 + sub], q[r0:r0 + sub], k[r0:r0 + sub]
        groups = [jnp.zeros((SUBLANES, cl), F32) for _ in range(sub // SUBLANES)]
        for s in range(sub):
            own = s // SUBLANES
            for gi in range(own, sub // SUBLANES):
                rows = slice(gi * SUBLANES, (gi + 1) * SUBLANES)
                rel = b_i[rows] - b_i[s:s + 1]
                if gi == own:
                    rel = jnp.where(tsub >= s - gi * SUBLANES, rel, -jnp.inf)
                w = jnp.sum(q_i[rows] * k_i[s:s + 1] * jnp.exp(rel), axis=1, keepdims=True)
                groups[gi] = jnp.where(acol == r0 + s, w, groups[gi])
        a_i = jnp.concatenate(groups, axis=0)
        if i > 0:
            ref = b[r0 - 1:r0]
            q_t = (q_i * jnp.exp(b_i - ref)).astype(BF16)
            k_t = jnp.concatenate([k[:r0] * jnp.exp(ref - b[:r0]), jnp.zeros((cl - r0, dk), F32)], axis=0)
            a_i = a_i + _dot_nt(q_t, k_t.astype(BF16))
        blocks.append(a_i)
    attn = jnp.concatenate(blocks, axis=0).astype(BF16)
    o = o + _dot(attn, v)

    k_d = (k * jnp.exp(b_end - b)).astype(BF16)
    st_new = st * jnp.exp(b_end) + _dot_tn(v, k_d)
    return o, st_new


def _gla_body(cl, q_ref, k_ref, v_ref, la_ref, s0_ref, o_ref, sf_ref, st_ref):
    t = pl.program_id(1)
    heads, dv, dk = st_ref.shape

    @pl.when(t == 0)
    def _():
        for hh in range(heads):
            st_ref[hh] = s0_ref[0, hh].T

    n_chunks = q_ref.shape[1] // cl

    def chunk(ci, carry):
        r = pl.multiple_of(ci * cl, cl)
        for hh in range(heads):
            ks = slice(hh * dk, (hh + 1) * dk)
            vs = slice(hh * dv, (hh + 1) * dv)
            o, st = _gla_chunk(q_ref[0, pl.ds(r, cl), ks], k_ref[0, pl.ds(r, cl), ks], v_ref[0, pl.ds(r, cl), vs],
                               la_ref[0, pl.ds(r, cl), ks], st_ref[hh])
            o_ref[0, pl.ds(r, cl), vs] = o.astype(o_ref.dtype)
            st_ref[hh] = st
        return carry

    lax.fori_loop(0, n_chunks, chunk, 0, unroll=2 if n_chunks % 2 == 0 else 1)

    @pl.when(t == pl.num_programs(1) - 1)
    def _():
        for hh in range(heads):
            sf_ref[0, hh] = st_ref[hh].T


def _gla(gq, gk, gv, la, s0, s0_off=0, tt=512):
    b, t, _ = gq.shape
    _, h, dk, dv = s0.shape
    cl = min(GLA_CHUNK, t)
    tt = _row_tile(t, tt)
    return pl.pallas_call(
        functools.partial(_gla_body, cl),
        grid=(b, t // tt),
        in_specs=[pl.BlockSpec((1, tt, h * dk), lambda i, n: (i, n, 0)),
                  pl.BlockSpec((1, tt, h * dk), lambda i, n: (i, n, 0)),
                  pl.BlockSpec((1, tt, h * dv), lambda i, n: (i, n, 0)),
                  pl.BlockSpec((1, tt, h * dk), lambda i, n: (i, n, 0)),
                  pl.BlockSpec((1, h, dk, dv), lambda i, n: (i + s0_off, 0, 0, 0))],
        out_specs=[pl.BlockSpec((1, tt, h * dv), lambda i, n: (i, n, 0)),
                   pl.BlockSpec((1, h, dk, dv), lambda i, n: (i, 0, 0, 0))],
        out_shape=[jax.ShapeDtypeStruct((b, t, h * dv), BF16), jax.ShapeDtypeStruct((b, h, dk, dv), F32)],
        scratch_shapes=[pltpu.VMEM((h, dv, dk), F32)],
        compiler_params=_params(("parallel", "arbitrary")),
        name="gla",
    )(gq, gk, gv, la, s0)


def _fox_prompt_body(hps, q_ref, k_ref, v_ref, o_ref, m_ref, acc_ref):
    qi = pl.program_id(2)
    tq = q_ref.shape[1]
    hw = q_ref.shape[2] // hps
    hd = v_ref.shape[2] // hps
    wide = FOX_WIDE * tq

    def logits(hh, r, width):
        sl = slice(hh * hw, (hh + 1) * hw)
        return _dot_nt(q_ref[0, :, sl], k_ref[0, pl.ds(r, width), sl])

    def values(hh, r, width):
        lane = lax.broadcasted_iota(jnp.int32, (width, LANES), 1)
        ones_col = jnp.where(lane == 0, 1.0, 0.0).astype(BF16)
        return jnp.concatenate([v_ref[0, pl.ds(r, width), hh * hd:(hh + 1) * hd], ones_col], axis=1)

    def first(r, width):
        row = lax.broadcasted_iota(jnp.int32, (tq, width), 0)
        col = lax.broadcasted_iota(jnp.int32, (tq, width), 1)
        for hh in range(hps):
            z = jnp.where(col <= row + (width - tq), logits(hh, r, width), -jnp.inf)
            m = jnp.max(z, axis=1, keepdims=True)
            m_ref[hh] = m
            acc_ref[hh] = _dot(jnp.exp2(z - m).astype(BF16), values(hh, r, width))

    assert FOX_WIDE == 2
    lead = jnp.where(qi % 2 == 1, 1, jnp.where(qi >= 2, 2, 0))

    for extra_blocks in range(3):
        @pl.when(lead == extra_blocks)
        def _(extra_blocks=extra_blocks):
            first(pl.multiple_of((qi - extra_blocks) * tq, tq), (extra_blocks + 1) * tq)

    def update(r, width):
        for hh in range(hps):
            z = logits(hh, r, width)
            m_old = m_ref[hh]
            m_new = jnp.maximum(m_old, jnp.max(z, axis=1, keepdims=True))
            m_ref[hh] = m_new
            pv = _dot(jnp.exp2(z - m_new).astype(BF16), values(hh, r, width))
            acc_ref[hh] = jnp.exp2(m_old - m_new) * acc_ref[hh] + pv

    n_wide = (qi - lead) // FOX_WIDE

    @pl.when(n_wide % 2 == 1)
    def _():
        update(0, wide)

    base = (n_wide % 2) * wide

    @pl.when((n_wide // 2) % 2 == 1)
    def _():
        update(pl.multiple_of(base, wide), 2 * wide)

    base = base + ((n_wide // 2) % 2) * 2 * wide

    def quad_wide_block(j, carry):
        update(pl.multiple_of(base + j * 4 * wide, wide), 4 * wide)
        return carry

    lax.fori_loop(0, n_wide // 4, quad_wide_block, 0)
    for hh in range(hps):
        acc = acc_ref[hh]
        o_ref[0, :, hh * hd:(hh + 1) * hd] = (acc[:, :hd] / acc[:, hd:hd + 1]).astype(o_ref.dtype)


def _fox_prompt(q_aug, k_aug, fv, tq=512, hps=FOX_HEADS_PER_STEP):
    b, t, d = fv.shape
    h = FOX_HEADS
    hd = d // h
    nq = t // tq
    keys = pl.BlockSpec((1, t, hps * 2 * hd), lambda i, j, n: (i, 0, j), pipeline_mode=pl.Buffered(1))
    vals = pl.BlockSpec((1, t, hps * hd), lambda i, j, n: (i, 0, j))
    return pl.pallas_call(
        functools.partial(_fox_prompt_body, hps),
        grid=(b, h // hps, nq),
        in_specs=[pl.BlockSpec((1, tq, hps * 2 * hd), lambda i, j, n: (i, n, j)), keys, vals],
        out_specs=pl.BlockSpec((1, tq, hps * hd), lambda i, j, n: (i, n, j)),
        out_shape=jax.ShapeDtypeStruct((b, t, d), BF16),
        scratch_shapes=[pltpu.VMEM((hps, tq, 1), F32), pltpu.VMEM((hps, tq, hd + LANES), F32)],
        compiler_params=_params(("parallel", "parallel", "arbitrary")),
        name="fox_prompt",
    )(q_aug, k_aug, fv)


def _fox_sample_body(heads, q_ref, kc_ref, vc_ref, kn_ref, vn_ref, fq_ref, fkc_ref, fkn_ref, o_ref,
                     m_ref, l_ref, acc_ref):
    c = pl.program_id(1)
    q = q_ref[0]
    rows = q.shape[0]
    ln = rows // heads
    f_q = fq_ref[0] * LOG2E
    head_q = lax.broadcasted_iota(jnp.int32, (rows, 1), 0) // ln

    def head_k(n):
        return lax.broadcasted_iota(jnp.int32, (1, n), 1) % heads

    @pl.when(c == 0)
    def _():
        n = kn_ref.shape[1]
        z = _dot_nt(q, kn_ref[0]) + f_q - fkn_ref[0] * LOG2E
        step_k = lax.broadcasted_iota(jnp.int32, (1, n), 1) // heads
        step_q = lax.broadcasted_iota(jnp.int32, (rows, 1), 0) % ln
        z = jnp.where((head_k(n) == head_q) & (step_k <= step_q), z, -jnp.inf)
        m = jnp.max(z, axis=1, keepdims=True)
        p = jnp.exp2(z - m)
        m_ref[...] = m
        l_ref[...] = jnp.sum(p, axis=1, keepdims=True)
        acc_ref[...] = _dot(p.astype(BF16), vn_ref[0])

    n = kc_ref.shape[1]
    z = _dot_nt(q, kc_ref[0].astype(BF16)) + f_q - fkc_ref[0] * LOG2E
    z = jnp.where(head_k(n) == head_q, z, -jnp.inf)
    m_old = m_ref[...]
    m_new = jnp.maximum(m_old, jnp.max(z, axis=1, keepdims=True))
    alpha = jnp.exp2(m_old - m_new)
    p = jnp.exp2(z - m_new)
    m_ref[...] = m_new
    l_ref[...] = alpha * l_ref[...] + jnp.sum(p, axis=1, keepdims=True)
    acc_ref[...] = alpha * acc_ref[...] + _dot(p.astype(BF16), vc_ref[0].astype(BF16))

    @pl.when(c == pl.num_programs(1) - 1)
    def _():
        o_ref[0] = (acc_ref[...] / l_ref[...]).astype(o_ref.dtype)


def _fox_sample(fq, fk, fv, cache_k, cache_v, cache_off, f_all, chunk_steps=1024):
    b, ln, d = fq.shape
    rows_total, past, h, hd = cache_k.shape
    assert h & (h - 1) == 0 and ln & (ln - 1) == 0
    chunk_steps = _row_tile(past, chunk_steps)
    ch = chunk_steps * h
    q_rows = fq.reshape(b, ln, h, hd).transpose(0, 2, 1, 3).reshape(b, h * ln, hd)
    f_q = jnp.swapaxes(f_all[:, past:, :], 1, 2).reshape(b, h * ln, 1)
    f_k = f_all.reshape(b, 1, (past + ln) * h)
    new = pl.BlockSpec((1, ln * h, hd), lambda i, c: (i, 0, 0))
    cache = pl.BlockSpec((1, ch, hd), lambda i, c: (i + cache_off, c, 0))
    o = pl.pallas_call(
        functools.partial(_fox_sample_body, h),
        grid=(b, past // chunk_steps),
        in_specs=[new, cache, cache, new, new,
                  pl.BlockSpec((1, h * ln, 1), lambda i, c: (i, 0, 0)),
                  pl.BlockSpec((1, 1, ch), lambda i, c: (i, 0, c)),
                  pl.BlockSpec((1, 1, ln * h), lambda i, c: (i, 0, 0))],
        out_specs=new,
        out_shape=jax.ShapeDtypeStruct((b, h * ln, hd), BF16),
        scratch_shapes=[pltpu.VMEM((h * ln, 1), F32), pltpu.VMEM((h * ln, 1), F32), pltpu.VMEM((h * ln, hd), F32)],
        compiler_params=_params(("parallel", "arbitrary")),
        name="fox_sample",
    )(q_rows, cache_k.reshape(rows_total, past * h, hd), cache_v.reshape(rows_total, past * h, hd),
      fk.reshape(b, ln * h, hd), fv.reshape(b, ln * h, hd), f_q, f_k[:, :, :past * h], f_k[:, :, past * h:])
    return o.reshape(b, h, ln, hd).transpose(0, 2, 1, 3).reshape(b, ln, d)


def _mem_kv_body(mem_ref, g_ref, w_ref, k_ref, v_ref, k16_ref, v16_ref):
    u = _rmsnorm(mem_ref[0], g_ref[...]).astype(BF16)
    _, _, heads, hd = k_ref.shape
    d = heads * hd
    k = _dot(u, w_ref[:, :d])
    v = _dot(u, w_ref[:, d:])
    k16_ref[0] = k.astype(BF16)
    v16_ref[0] = v.astype(BF16)
    for hh in range(heads):
        k_ref[0, :, hh, :] = k[:, hh * hd:(hh + 1) * hd]
        v_ref[0, :, hh, :] = v[:, hh * hd:(hh + 1) * hd]


def _mem_kv(mem, g, w):
    b, n, d = mem.shape
    hd = d // MEM_HEADS
    flat = pl.BlockSpec((1, n, d), lambda i: (i, 0, 0))
    out = pl.BlockSpec((1, n, MEM_HEADS, hd), lambda i: (i, 0, 0, 0))
    return pl.pallas_call(
        _mem_kv_body,
        grid=(b,),
        in_specs=[flat, _resident(g.shape), _resident(w.shape)],
        out_specs=[out, out, flat, flat],
        out_shape=[jax.ShapeDtypeStruct((b, n, MEM_HEADS, hd), F32)] * 2 + [jax.ShapeDtypeStruct((b, n, d), BF16)] * 2,
        compiler_params=_params(("parallel",)),
        name="mem_kv",
    )(mem, g, w)


def _mix_out_body(seq, x_ref, og_ref, of_ref, mk_ref, mv_ref, pre_ref, w2_ref, gn_ref, wgo_ref, wfo_ref, wmo_ref,
                  wout_ref, post_ref, o_ref, u_ref):
    x = x_ref[...]
    tm, d = x.shape
    u_ref[...] = _rmsnorm(x, pre_ref[...]).astype(BF16)

    def proj(j):
        return _dot(u_ref[...], w2_ref[:, j * d:(j + 1) * d])

    mq = proj(1)
    hd = d // MEM_HEADS
    scale = float(hd ** -0.5)
    per_batch = []
    for j in range(tm // seq):
        heads = []
        for hh in range(MEM_HEADS):
            sl = slice(hh * hd, (hh + 1) * hd)
            s = _dot_nt(mq[j * seq:(j + 1) * seq, sl].astype(BF16), mk_ref[j, :, sl]) * scale
            e = jnp.exp(s - jnp.max(s, axis=1, keepdims=True))
            p = e / jnp.sum(e, axis=1, keepdims=True)
            heads.append(_dot(p.astype(BF16), mv_ref[j, :, sl]))
        per_batch.append(jnp.concatenate(heads, axis=1))
    o_mem = per_batch[0] if len(per_batch) == 1 else jnp.concatenate(per_batch, axis=0)
    y = _sigmoid(proj(4)) * _dot(o_mem.astype(BF16), wmo_ref[...])

    y = y + _sigmoid(proj(3)) * _dot(of_ref[...], wfo_ref[...])

    og = og_ref[...].astype(F32)
    gd = d // GLA_HEADS
    gn = gn_ref[...]
    normed = []
    for hh in range(GLA_HEADS):
        sl = slice(hh * gd, (hh + 1) * gd)
        normed.append(_rmsnorm(og[:, sl], gn[:, sl]))
    r = proj(0)
    b_gla = _dot((jnp.concatenate(normed, axis=1) * (r * _sigmoid(r))).astype(BF16), wgo_ref[...])
    y = y + _sigmoid(proj(2)) * b_gla

    mixed = _dot(y.astype(BF16), wout_ref[...])
    o_ref[...] = x + _rmsnorm(mixed, post_ref[...])


def _mix_out(x2, o_gla, o_fox, mk, mv, mem_off, seq, pre_g, w2, gn, wgo, wfo, wmo, wout, post_g, tm=512):
    m, d = x2.shape
    if seq < tm:
        nb = 4
        while (m // seq) % nb:
            nb //= 2
        tm = seq * nb
    assert m % tm == 0 and (seq % tm == 0 or tm % seq == 0)
    nb = max(1, tm // seq)
    steps_per_batch = max(1, seq // tm)
    mem = mk.shape[1]
    rows = pl.BlockSpec((tm, d), lambda i: (i, 0))
    assert mem_off % nb == 0
    memb = pl.BlockSpec((nb, mem, d), lambda i: (i // steps_per_batch + mem_off // nb, 0, 0))
    return pl.pallas_call(
        functools.partial(_mix_out_body, min(seq, tm)),
        grid=(m // tm,),
        in_specs=[rows, rows, rows, memb, memb, _resident(pre_g.shape), _resident(w2.shape), _resident(gn.shape),
                  _resident(wgo.shape), _resident(wfo.shape), _resident(wmo.shape), _resident(wout.shape),
                  _resident(post_g.shape)],
        out_specs=rows,
        out_shape=jax.ShapeDtypeStruct((m, d), F32),
        scratch_shapes=[pltpu.VMEM((tm, d), BF16)],
        compiler_params=_params(("parallel",)),
        name="mix_out",
    )(x2, o_gla, o_fox, mk, mv, pre_g, w2, gn, wgo, wfo, wmo, wout, post_g)


def _prep_layer(d, w):
    gk_w = d // 2
    sizes = (gk_w, gk_w, d, GLA_RANK, d, d, d, d, FOX_HEADS, d, N_BRANCH * d)
    offs = np.concatenate([[0], np.cumsum(sizes)]).tolist()
    seg = {n: (offs[i], offs[i + 1]) for i, n in enumerate(
        ("gq", "gk", "gv", "glr", "gr", "fq", "fk", "fv", "ff", "mq", "gates"))}
    w_in = w["w_in"]

    def cols(*names):
        return jnp.concatenate([w_in[:, seg[n][0]:seg[n][1]] for n in names], axis=1).astype(BF16)

    ws = jnp.pad(cols("ff", "glr", "ff", "ff"), ((0, 0), (0, SMALL_W - 3 * FOX_HEADS - GLA_RANK)))
    wa2p = jnp.pad(w["gla_w_a2"].astype(BF16), ((FOX_HEADS, SMALL_W - FOX_HEADS - GLA_RANK), (0, 0)))
    b_f = jnp.concatenate([w["fox_b_f"], jnp.zeros((GLA_RANK,), F32), w["fox_b_f"], w["fox_b_f"]])
    b_f = jnp.pad(b_f, (0, SMALL_W - b_f.shape[0])).reshape(1, SMALL_W)

    def ffn(pfx):
        wg, wu, wd = w[pfx + "_w_gate"], w[pfx + "_w_up"], w[pfx + "_w_down"]
        assert wg.shape[1] % FFN_CHUNK == 0
        return dict(pre=w[pfx + "_pre_g"].reshape(1, d), post=w[pfx + "_post_g"].reshape(1, d),
                    wg=wg.astype(BF16), wu=wu.astype(BF16), wd=wd.astype(BF16))

    return dict(
        ffn1=ffn("ffn1"), ffn2=ffn("ffn2"),
        mix_pre=w["mix_pre_g"].reshape(1, d), mix_post=w["mix_post_g"].reshape(1, d),
        w1=cols("gq", "gk", "gv", "fq", "fk", "fv"), ws=ws, wa2p=wa2p,
        b_a=w["gla_b_a"].reshape(1, gk_w), b_f=b_f,
        w2=cols("gr", "mq", "gates"),
        gn=w["gla_norm_g"].reshape(1, d),
        wgo=w["w_gla_o"].astype(BF16), wfo=w["w_fox_o"].astype(BF16), wmo=w["w_mem_o"].astype(BF16),
        wout=w["w_out"].astype(BF16),
        mem_g=w["mem_norm_g"].reshape(1, d), w_mem_kv=w["w_mem_kv"].astype(BF16),
    )


def _ffn_apply(x2, p):
    return _ffn(x2, p["pre"], p["post"], p["wg"], p["wu"], p["wd"])


def _forget_layouts(f, lead=None):
    b, t, h = f.shape
    if lead is not None:
        f = jnp.concatenate([lead.astype(F32), f], axis=1)
    total = f.shape[1]
    padded = -(-total // LANES) * LANES
    f = jnp.pad(f, ((0, 0), (0, padded - total), (0, 0)))
    rows = _cumsum_time(jnp.swapaxes(f, 1, 2).reshape(b * h, padded // LANES, LANES))
    rows = rows.reshape(b, h, padded)
    cols = jnp.swapaxes(rows[:, :, total - t:total], 1, 2)
    return cols, rows


def _group_layer(x2, b, t, p, layer, depth, carried, gla_state, fox, memory, augment):
    d = x2.shape[1]
    x2 = _ffn_apply(x2, p["ffn1"])
    gq, gk, gv, la, fq, fk16, fv16, *carried = _mix_in(x2, p["mix_pre"], p["w1"], p["ws"], p["wa2p"], p["b_a"],
                                                       p["b_f"], layer, depth, carried,
                                                       seq=t if augment else None)
    r3 = lambda a: a.reshape(b, t, a.shape[1])
    o_gla, s_fin = _gla(r3(gq), r3(gk), r3(gv), r3(la), gla_state[0], gla_state[1])
    f_log = carried[2][layer].reshape(b, t, SMALL_W)[:, :, :FOX_HEADS]
    o_fox = fox(r3(fq), r3(fk16), r3(fv16), f_log)
    x2 = _mix_out(x2, o_gla.reshape(b * t, d), o_fox.reshape(b * t, d), memory[0], memory[1], memory[2], t,
                  p["mix_pre"], p["w2"], p["gn"], p["wgo"], p["wfo"], p["wmo"], p["wout"], p["mix_post"])
    x2 = _ffn_apply(x2, p["ffn2"])
    return x2, carried, s_fin


def kernel(x_prompt, x_sample, cache_fox_k, cache_fox_v, cache_fox_logf, state_gla, cache_mem_k, cache_mem_v, mem_prompt, ffn1_pre_g, ffn1_post_g, ffn1_w_gate, ffn1_w_up, ffn1_w_down, mix_pre_g, mix_post_g, w_in, gla_w_a2, gla_b_a, fox_b_f, gla_norm_g, w_gla_o, w_fox_o, w_mem_o, w_out, mem_norm_g, w_mem_kv, ffn2_pre_g, ffn2_post_g, ffn2_w_gate, ffn2_w_up, ffn2_w_down):
    weights = dict(ffn1_pre_g=ffn1_pre_g, ffn1_post_g=ffn1_post_g, ffn1_w_gate=ffn1_w_gate, ffn1_w_up=ffn1_w_up,
                   ffn1_w_down=ffn1_w_down, mix_pre_g=mix_pre_g, mix_post_g=mix_post_g, w_in=w_in,
                   gla_w_a2=gla_w_a2, gla_b_a=gla_b_a, fox_b_f=fox_b_f, gla_norm_g=gla_norm_g, w_gla_o=w_gla_o,
                   w_fox_o=w_fox_o, w_mem_o=w_mem_o, w_out=w_out, mem_norm_g=mem_norm_g, w_mem_kv=w_mem_kv,
                   ffn2_pre_g=ffn2_pre_g, ffn2_post_g=ffn2_post_g, ffn2_w_gate=ffn2_w_gate, ffn2_w_up=ffn2_w_up,
                   ffn2_w_down=ffn2_w_down)
    depth = w_in.shape[0]
    bp, tp, d = x_prompt.shape
    bs, ts, _ = x_sample.shape
    hd = d // FOX_HEADS
    dk, dv = state_gla.shape[-2:]
    layers = [_prep_layer(d, {k: v[l] for k, v in weights.items()}) for l in range(depth)]

    x = x_prompt.reshape(bp * tp, d)
    zero_state = jnp.zeros((bp, GLA_HEADS, dk, dv), F32)
    tq = min(512, tp)

    def fox_p(q_aug, k_aug, fv, f_log):
        return _fox_prompt(q_aug, k_aug, fv, tq=tq)

    carried, p_state, p_mk, p_mv = [], [], [], []
    for l in range(depth):
        mk, mv, mk16, mv16 = _mem_kv(mem_prompt, layers[l]["mem_g"], layers[l]["w_mem_kv"])
        x, carried, s_fin = _group_layer(x, bp, tp, layers[l], l, depth, carried, (zero_state, 0), fox_p,
                                         (mk16, mv16, 0), True)
        p_state.append(s_fin)
        p_mk.append(mk)
        p_mv.append(mv)
    y_prompt = x.reshape(bp, tp, d)
    p_fox_k = carried[0].reshape(depth, bp, tp, FOX_HEADS, hd)
    p_fox_v = carried[1].reshape(depth, bp, tp, FOX_HEADS, hd)
    p_fox_logf = carried[2].reshape(depth, bp, tp, SMALL_W)[..., :FOX_HEADS]
    p_mem_k = jnp.stack(p_mk)
    p_mem_v = jnp.stack(p_mv)

    x = x_sample.reshape(bs * ts, d)
    past = cache_fox_k.shape[2]
    ck = cache_fox_k.reshape(depth * bs, past, FOX_HEADS, hd)
    cv = cache_fox_v.reshape(depth * bs, past, FOX_HEADS, hd)
    states = state_gla.reshape(depth * bs, GLA_HEADS, dk, dv)
    mem_s = cache_mem_k.shape[2]
    cmk = cache_mem_k.reshape(depth * bs, mem_s, d).astype(BF16)
    cmv = cache_mem_v.reshape(depth * bs, mem_s, d).astype(BF16)
    carried, s_state = [], []
    for l in range(depth):
        def fox_s(fq, fk, fv, f_log, l=l):
            _, f_rows = _forget_layouts(f_log, lead=cache_fox_logf[l])
            f_all = jnp.swapaxes(f_rows[:, :, :past + ts], 1, 2)
            return _fox_sample(fq, fk, fv, ck, cv, l * bs, f_all)

        x, carried, s_new = _group_layer(x, bs, ts, layers[l], l, depth, carried, (states, l * bs), fox_s,
                                         (cmk, cmv, l * bs), False)
        s_state.append(s_new)
    y_sample = x.reshape(bs, ts, d)
    s_fox_k = carried[0].reshape(depth, bs, ts, FOX_HEADS, hd)
    s_fox_v = carried[1].reshape(depth, bs, ts, FOX_HEADS, hd)
    s_fox_logf = carried[2].reshape(depth, bs, ts, SMALL_W)[..., :FOX_HEADS]

    return (y_prompt, y_sample, p_fox_k, p_fox_v, p_fox_logf, jnp.stack(p_state), p_mem_k, p_mem_v,
            s_fox_k, s_fox_v, s_fox_logf, jnp.stack(s_state))
```
